```python
import jax
import jax.numpy as jnp
from jax import lax
import numpy as np

D_MODEL = 2048
BATCH = 4
SEQ = 4096
DEPTH = 2

MEM_LEN = 256
HEAD_DIM = 128
FOX_HEADS = 8
NSA_HEADS = 8
NSA_KV_HEADS = 2
NSA_GROUP = NSA_HEADS // NSA_KV_HEADS
MEM_HEADS = 4
MEM_HEAD_DIM = 256
FOX_W = FOX_HEADS * HEAD_DIM
NSA_W = NSA_HEADS * HEAD_DIM
NSA_KV_W = NSA_KV_HEADS * HEAD_DIM
MEM_W = MEM_HEADS * MEM_HEAD_DIM
N_BRANCHES = 3
CMP_LEN = 32
CMP_STRIDE = 16
SLC_BLOCK = 64
SLC_TOPK = 16
WINDOW = 512
Q_BLOCK = 128
SLC_Q_BLOCK = 64
ROPE_THETA = 500000.0
ROPE_DIM = HEAD_DIM // 4
D_FF = 4 * D_MODEL
RMS_EPS = 1e-6
NEG = -1e30
FORCED_SCORE = 1e6
IN_SPLITS = (FOX_W, FOX_W, FOX_W, FOX_HEADS, NSA_W, NSA_KV_W, NSA_KV_W, NSA_KV_W, NSA_KV_W, NSA_KV_W, NSA_KV_W, 3 * NSA_HEADS, MEM_W, N_BRANCHES * D_MODEL)
D_IN = 3 * FOX_W + FOX_HEADS + NSA_W + 6 * NSA_KV_W + 3 * NSA_HEADS + MEM_W + N_BRANCHES * D_MODEL

kernel_name = "hybrid_fox_nsa_memory_block"


def rms_norm(x, g):
    xf = x.astype(jnp.float32)
    y = xf * lax.rsqrt(jnp.mean(xf * xf, axis=-1, keepdims=True) + RMS_EPS)
    return (y * g.astype(jnp.float32)).astype(x.dtype)


def to_heads(x, n_heads, head_dim):
    b, t, _ = x.shape
    return x.reshape(b, t, n_heads, head_dim).transpose(0, 2, 1, 3)


def from_heads(x):
    b, h, t, d = x.shape
    return x.transpose(0, 2, 1, 3).reshape(b, t, h * d)


def rope_partial(x, pos):
    half = ROPE_DIM // 2
    inv_freq = ROPE_THETA ** (-jnp.arange(half, dtype=jnp.float32) / half)
    ang = pos.astype(jnp.float32)[:, None] * inv_freq[None, :]
    cos, sin = jnp.cos(ang), jnp.sin(ang)
    xr = x[..., :ROPE_DIM].astype(jnp.float32)
    x1, x2 = xr[..., :half], xr[..., half:]
    rot = jnp.concatenate([x1 * cos - x2 * sin, x1 * sin + x2 * cos], axis=-1)
    return jnp.concatenate([rot.astype(x.dtype), x[..., ROPE_DIM:]], axis=-1)


def fox_attention(q, k, v, log_f):
    b, h, t, d = q.shape
    c = jnp.cumsum(log_f, axis=-1)
    kpos = jnp.arange(t)
    scale = d ** -0.5

    def block(i):
        s0 = i * Q_BLOCK
        qb = lax.dynamic_slice_in_dim(q, s0, Q_BLOCK, axis=2)
        cb = lax.dynamic_slice_in_dim(c, s0, Q_BLOCK, axis=2)
        qpos = s0 + jnp.arange(Q_BLOCK)
        logits = jnp.einsum('bhqd,bhkd->bhqk', qb, k, preferred_element_type=jnp.float32) * scale
        logits = logits + (cb[..., :, None] - c[..., None, :])
        logits = jnp.where(kpos[None, :] <= qpos[:, None], logits, -jnp.inf)
        p = jax.nn.softmax(logits, axis=-1)
        return jnp.einsum('bhqk,bhkd->bhqd', p.astype(v.dtype), v)

    out = lax.map(block, jnp.arange(t // Q_BLOCK))
    return jnp.moveaxis(out, 0, 2).reshape(b, h, t, d)


def compress(x, w1, w2, pe):
    b, g, t, d = x.shape
    nc = (t - CMP_LEN) // CMP_STRIDE + 1
    idx = np.arange(nc)[:, None] * CMP_STRIDE + np.arange(CMP_LEN)[None, :]
    blocks = x[:, :, idx, :] + pe
    hdn = jax.nn.gelu(blocks.reshape(b, g, nc, CMP_LEN * d) @ w1)
    return hdn @ w2


def nsa_attention(q, kc, vc, ks, vs, kw, vw, gate_logits, pos):
    b, hq, t, d = q.shape
    g = NSA_KV_HEADS
    scale = d ** -0.5
    qg = q.reshape(b, g, NSA_GROUP, t, d)

    nc = kc.shape[2]
    cmp_end = np.arange(nc) * CMP_STRIDE + CMP_LEN - 1
    cmp_valid = jnp.asarray(cmp_end)[None, :] <= pos[:, None]
    logits = jnp.einsum('bgqtd,bgnd->bgqtn', qg, kc, preferred_element_type=jnp.float32) * scale
    logits = jnp.where(cmp_valid, logits, NEG)
    p_cmp = jnp.where(cmp_valid, jax.nn.softmax(logits, axis=-1), 0.0)
    o_cmp = jnp.einsum('bgqtn,bgnd->bgqtd', p_cmp.astype(vc.dtype), vc).reshape(b, hq, t, d)

    nsel = t // SLC_BLOCK
    k_top = min(SLC_TOPK, nsel)
    ci = np.arange(nc)[:, None] * CMP_STRIDE
    sj = np.arange(nsel)[None, :] * SLC_BLOCK
    overlap = jnp.asarray(((ci <= sj + SLC_BLOCK - 1) & (ci + CMP_LEN - 1 >= sj)).astype(np.float32))
    p_slc = jnp.einsum('bgtn,ns->bgts', p_cmp.sum(axis=2), overlap)
    blk = jnp.arange(nsel)[None, :]
    cur = (pos // SLC_BLOCK)[:, None]
    sel_valid = blk <= cur
    forced = (blk == 0) | (blk == cur) | (blk == cur - 1)
    score = jnp.where(forced, FORCED_SCORE, jnp.where(sel_valid, p_slc, -1.0))
    _, sel_idx = lax.top_k(score, k_top)

    kb = ks.reshape(b, g, nsel, SLC_BLOCK, d)
    vb = vs.reshape(b, g, nsel, SLC_BLOCK, d)
    gather = jax.vmap(jax.vmap(lambda blocks, ix: blocks[ix]))
    n_tok = k_top * SLC_BLOCK

    def slc_block(i):
        s0 = i * SLC_Q_BLOCK
        qc = lax.dynamic_slice_in_dim(qg, s0, SLC_Q_BLOCK, axis=3)
        ic = lax.dynamic_slice_in_dim(sel_idx, s0, SLC_Q_BLOCK, axis=2)
        kg = gather(kb, ic).reshape(b, g, SLC_Q_BLOCK, n_tok, d)
        vg = gather(vb, ic).reshape(b, g, SLC_Q_BLOCK, n_tok, d)
        tok = (ic[..., None] * SLC_BLOCK + jnp.arange(SLC_BLOCK)).reshape(b, g, SLC_Q_BLOCK, n_tok)
        qpos = s0 + jnp.arange(SLC_Q_BLOCK)
        mask = (tok <= qpos[:, None])[:, :, None]
        lg = jnp.einsum('bgqtd,bgtnd->bgqtn', qc, kg, preferred_element_type=jnp.float32) * scale
        p = jax.nn.softmax(jnp.where(mask, lg, -jnp.inf), axis=-1)
        return jnp.einsum('bgqtn,bgtnd->bgqtd', p.astype(vg.dtype), vg)

    o_slc = jnp.moveaxis(lax.map(slc_block, jnp.arange(t // SLC_Q_BLOCK)), 0, 3).reshape(b, hq, t, d)

    kp = jnp.pad(kw, ((0, 0), (0, 0), (WINDOW, 0), (0, 0)))
    vp = jnp.pad(vw, ((0, 0), (0, 0), (WINDOW, 0), (0, 0)))
    span = WINDOW + Q_BLOCK

    def win_block(i):
        s0 = i * Q_BLOCK
        qb = lax.dynamic_slice_in_dim(qg, s0, Q_BLOCK, axis=3)
        kbk = lax.dynamic_slice_in_dim(kp, s0, span, axis=2)
        vbk = lax.dynamic_slice_in_dim(vp, s0, span, axis=2)
        qpos = s0 + jnp.arange(Q_BLOCK)
        kpos = s0 - WINDOW + jnp.arange(span)
        rel = qpos[:, None] - kpos[None, :]
        mask = (rel >= 0) & (rel < WINDOW) & (kpos[None, :] >= 0)
        lg = jnp.einsum('bgqtd,bgkd->bgqtk', qb, kbk, preferred_element_type=jnp.float32) * scale
        p = jax.nn.softmax(jnp.where(mask, lg, -jnp.inf), axis=-1)
        return jnp.einsum('bgqtk,bgkd->bgqtd', p.astype(vbk.dtype), vbk)

    o_win = jnp.moveaxis(lax.map(win_block, jnp.arange(t // Q_BLOCK)), 0, 3).reshape(b, hq, t, d)

    gates = jax.nn.sigmoid(gate_logits).reshape(b, t, 3, hq).transpose(2, 0, 3, 1)[..., None]
    return gates[0] * o_cmp + gates[1] * o_slc + gates[2] * o_win


def memory_attention(q, mk, mv):
    scale = q.shape[-1] ** -0.5
    lg = jnp.einsum('bhtd,bhmd->bhtm', q, mk, preferred_element_type=jnp.float32) * scale
    p = jax.nn.softmax(lg, axis=-1)
    return jnp.einsum('bhtm,bhmd->bhtd', p.astype(mv.dtype), mv)


def token_mixer(h, mem, w_in, b_f, cmp_k_params, cmp_v_params, w_mem_kv, g_mem,
                w_up_fox, w_up_nsa, w_up_mem, w_o, g_pre, g_post):
    b, t, _ = h.shape
    pos = jnp.arange(t, dtype=jnp.int32)
    u = rms_norm(h, g_pre)
    proj = u @ w_in
    (fq, fk, fv, ff, nq, ck, cv, sk, sv, wk, wv, ng, mq, bg) = jnp.split(
        proj, np.cumsum(IN_SPLITS)[:-1].tolist(), axis=-1)

    log_f = jax.nn.log_sigmoid(ff.astype(jnp.float32) + b_f.astype(jnp.float32)).transpose(0, 2, 1)
    o_fox = fox_attention(to_heads(fq, FOX_HEADS, HEAD_DIM), to_heads(fk, FOX_HEADS, HEAD_DIM),
                          to_heads(fv, FOX_HEADS, HEAD_DIM), log_f)

    def kvh(z):
        return to_heads(z, NSA_KV_HEADS, HEAD_DIM)
    q_nsa = rope_partial(to_heads(nq, NSA_HEADS, HEAD_DIM), pos)
    kc = compress(rope_partial(kvh(ck), pos), *cmp_k_params)
    vc = compress(kvh(cv), *cmp_v_params)
    o_nsa = nsa_attention(q_nsa, kc, vc, rope_partial(kvh(sk), pos), kvh(sv),
                          rope_partial(kvh(wk), pos), kvh(wv), ng, pos)

    mkv = rms_norm(mem, g_mem) @ w_mem_kv
    mk, mv = jnp.split(mkv, 2, axis=-1)
    o_mem = memory_attention(to_heads(mq, MEM_HEADS, MEM_HEAD_DIM), to_heads(mk, MEM_HEADS, MEM_HEAD_DIM),
                             to_heads(mv, MEM_HEADS, MEM_HEAD_DIM))

    gate = jax.nn.sigmoid(bg).reshape(b, t, N_BRANCHES, D_MODEL)
    merged = (gate[:, :, 0] * (from_heads(o_fox) @ w_up_fox)
              + gate[:, :, 1] * (from_heads(o_nsa) @ w_up_nsa)
              + gate[:, :, 2] * (from_heads(o_mem) @ w_up_mem))
    return rms_norm(merged @ w_o, g_post)


def channel_mixer(h, g_pre, w1, w2, g_post):
    u = rms_norm(h, g_pre)
    a = jnp.square(jax.nn.relu(u @ w1))
    return rms_norm(a @ w2, g_post)


def setup_inputs(seed: int = 0) -> dict:
    key = jax.random.key(seed)
    ks = jax.random.split(key, 24)
    L = DEPTH

    def nrm(k, shape, fan_in):
        return jax.random.normal(k, shape, jnp.float32) * (fan_in ** -0.5)

    def gain(k, shape):
        return 1.0 + 0.05 * jax.random.normal(k, shape, jnp.float32)

    return {
        "x": jax.random.normal(ks[0], (BATCH, SEQ, D_MODEL), jnp.float32),
        "mem": jax.random.normal(ks[1], (BATCH, MEM_LEN, D_MODEL), jnp.float32),
        "w_in": nrm(ks[2], (L, D_MODEL, D_IN), D_MODEL),
        "b_f": jax.random.uniform(ks[3], (L, FOX_HEADS), jnp.float32, 1.0, 6.0),
        "w_cmp1_k": nrm(ks[4], (L, CMP_LEN * HEAD_DIM, HEAD_DIM), CMP_LEN * HEAD_DIM),
        "w_cmp2_k": nrm(ks[5], (L, HEAD_DIM, HEAD_DIM), HEAD_DIM),
        "pe_cmp_k": 0.1 * jax.random.normal(ks[6], (L, CMP_LEN, HEAD_DIM), jnp.float32),
        "w_cmp1_v": nrm(ks[7], (L, CMP_LEN * HEAD_DIM, HEAD_DIM), CMP_LEN * HEAD_DIM),
        "w_cmp2_v": nrm(ks[8], (L, HEAD_DIM, HEAD_DIM), HEAD_DIM),
        "pe_cmp_v": 0.1 * jax.random.normal(ks[9], (L, CMP_LEN, HEAD_DIM), jnp.float32),
        "w_mem_kv": nrm(ks[10], (L, D_MODEL, 2 * MEM_W), D_MODEL),
        "g_mem": gain(ks[11], (L, D_MODEL)),
        "w_up_fox": nrm(ks[12], (L, FOX_W, D_MODEL), FOX_W),
        "w_up_nsa": nrm(ks[13], (L, NSA_W, D_MODEL), NSA_W),
        "w_up_mem": nrm(ks[14], (L, MEM_W, D_MODEL), MEM_W),
        "w_o": nrm(ks[15], (L, D_MODEL, D_MODEL), D_MODEL),
        "g_pre_mix": gain(ks[16], (L, D_MODEL)),
        "g_post_mix": gain(ks[17], (L, D_MODEL)),
        "g_pre_mlp": gain(ks[18], (L, D_MODEL)),
        "g_post_mlp": gain(ks[19], (L, D_MODEL)),
        "w_mlp1": nrm(ks[20], (L, D_MODEL, D_FF), D_MODEL),
        "w_mlp2": nrm(ks[21], (L, D_FF, D_MODEL), D_FF),
    }


def reference(x, mem, w_in, b_f, w_cmp1_k, w_cmp2_k, pe_cmp_k, w_cmp1_v, w_cmp2_v, pe_cmp_v,
              w_mem_kv, g_mem, w_up_fox, w_up_nsa, w_up_mem, w_o, g_pre_mix, g_post_mix,
              g_pre_mlp, g_post_mlp, w_mlp1, w_mlp2):
    h = x
    for l in range(DEPTH):
        h = h + token_mixer(h, mem, w_in[l], b_f[l],
                            (w_cmp1_k[l], w_cmp2_k[l], pe_cmp_k[l]),
                            (w_cmp1_v[l], w_cmp2_v[l], pe_cmp_v[l]),
                            w_mem_kv[l], g_mem[l], w_up_fox[l], w_up_nsa[l], w_up_mem[l], w_o[l],
                            g_pre_mix[l], g_post_mix[l])
        h = h + channel_mixer(h, g_pre_mlp[l], w_mlp1[l], w_mlp2[l], g_post_mlp[l])
    return h
```

```python
import functools

import numpy as np
import jax
import jax.numpy as jnp
from jax import lax
from jax.experimental import pallas as pl
from jax.experimental.pallas import tpu as pltpu

F32 = jnp.float32
BF16 = jnp.bfloat16

HEAD_DIM = 128
FOX_HEADS = 8
NSA_HEADS = 8
NSA_KV_HEADS = 2
NSA_GROUP = NSA_HEADS // NSA_KV_HEADS
MEM_HEADS = 4
MEM_HEAD_DIM = 256
N_BRANCHES = 3
CMP_LEN = 32
CMP_STRIDE = 16
SLC_BLOCK = 64
SLC_TOPK = 16
WINDOW = 512
ROPE_THETA = 500000.0
ROPE_DIM = HEAD_DIM // 4
RMS_EPS = 1e-6
NEG = -1e30
FORCED_SCORE = 1e6

LANES = 128
VMEM_LIMIT = 56 * 1024 * 1024

FOX_W = FOX_HEADS * HEAD_DIM
NSA_W = NSA_HEADS * HEAD_DIM
NSA_KV_W = NSA_KV_HEADS * HEAD_DIM
MEM_W = MEM_HEADS * MEM_HEAD_DIM
OFF_FQ = 0
OFF_FK = OFF_FQ + FOX_W
OFF_FV = OFF_FK + FOX_W
OFF_NQ = OFF_FV + FOX_W
OFF_BG = OFF_NQ + NSA_W
SMALL_W = LANES
SMALL_FF = 0
SMALL_NG = FOX_HEADS


def _params(sem):
    return pltpu.CompilerParams(dimension_semantics=sem, vmem_limit_bytes=VMEM_LIMIT)


def _rms(x, g):
    return (x * lax.rsqrt(jnp.mean(x * x, axis=-1, keepdims=True) + RMS_EPS)) * g


def _dot(a, b):
    return jnp.dot(a, b, preferred_element_type=F32)


def _dot_nt(a, b):
    return lax.dot_general(a, b, (((1,), (1,)), ((), ())), preferred_element_type=F32)


def _norm_matmul_kernel(x_ref, g_ref, w_ref, ws_ref, o_ref, os_ref, u_ref):
    @pl.when(pl.program_id(1) == 0)
    def _():
        u_ref[...] = _rms(x_ref[...], g_ref[...]).astype(BF16)
        os_ref[...] = _dot(u_ref[...], ws_ref[...])

    o_ref[...] = _dot(u_ref[...], w_ref[...]).astype(o_ref.dtype)


def _norm_matmul(x, g, w, ws, tm, tn, name):
    m, d = x.shape
    n = w.shape[1]
    return pl.pallas_call(
        _norm_matmul_kernel,
        out_shape=(jax.ShapeDtypeStruct((m, n), BF16), jax.ShapeDtypeStruct((m, SMALL_W), F32)),
        grid=(m // tm, n // tn),
        in_specs=[
            pl.BlockSpec((tm, d), lambda i, j: (i, 0)),
            pl.BlockSpec((1, d), lambda i, j: (0, 0)),
            pl.BlockSpec((d, tn), lambda i, j: (0, j)),
            pl.BlockSpec((d, SMALL_W), lambda i, j: (0, 0)),
        ],
        out_specs=(
            pl.BlockSpec((tm, tn), lambda i, j: (i, j)),
            pl.BlockSpec((tm, SMALL_W), lambda i, j: (i, 0)),
        ),
        scratch_shapes=[pltpu.VMEM((tm, d), BF16)],
        compiler_params=_params(("parallel", "arbitrary")),
        name=name,
    )(x, g.reshape(1, d), w, ws)


def _fox_cumsum_kernel(bf_ref, ff_ref, c_ref):
    x = ff_ref[...] + bf_ref[pl.program_id(1)]
    lf = jnp.minimum(x, 0.0) - jnp.log1p(jnp.exp(-jnp.abs(x)))
    rows, lanes = lf.shape
    lane = lax.broadcasted_iota(jnp.int32, lf.shape, 1)
    row = lax.broadcasted_iota(jnp.int32, lf.shape, 0)
    y = lf
    s = 1
    while s < lanes:
        y = y + jnp.where(lane >= s, pltpu.roll(y, s, axis=1), 0.0)
        s *= 2
    tot = jnp.broadcast_to(y[:, lanes - 1:lanes], lf.shape)
    z = tot
    s = 1
    while s < rows:
        z = z + jnp.where(row >= s, pltpu.roll(z, s, axis=0), 0.0)
        s *= 2
    c_ref[...] = y + (z - tot)


def _fox_cumsum(ff, b_f):
    b, h, t = ff.shape
    rows = t // LANES
    out = pl.pallas_call(
        _fox_cumsum_kernel,
        out_shape=jax.ShapeDtypeStruct((b, h, rows, LANES), F32),
        grid=(b, h),
        in_specs=[
            pl.BlockSpec(memory_space=pltpu.SMEM),
            pl.BlockSpec((None, None, rows, LANES), lambda i, j: (i, j, 0, 0)),
        ],
        out_specs=pl.BlockSpec((None, None, rows, LANES), lambda i, j: (i, j, 0, 0)),
        compiler_params=_params(("parallel", "parallel")),
        name="fox_cumsum",
    )(b_f, ff.reshape(b, h, rows, LANES))
    return out.reshape(b, h, t)


def _softmax_step(carry, s, v):
    m, l, acc = carry
    m_new = jnp.maximum(m, jnp.max(s, axis=-1, keepdims=True))
    alpha = jnp.exp(m - m_new)
    p = jnp.exp(s - m_new)
    l = alpha * l + jnp.sum(p, axis=-1, keepdims=True)
    acc = alpha * acc + _dot(p.astype(BF16), v)
    return m_new, l, acc


def _softmax_init(rows, d):
    return (jnp.full((rows, 1), -jnp.inf, F32), jnp.zeros((rows, 1), F32), jnp.zeros((rows, d), F32))


def _fox_attn_kernel(q_ref, k_ref, v_ref, c_ref, o_ref, *, tile, scale):
    i = pl.program_id(2)
    q = q_ref[...]

    def logits(kc):
        k = k_ref[pl.ds(pl.multiple_of(kc * tile, tile), tile), :]
        return _dot_nt(q, k) * scale - c_ref[kc]

    def body(kc, carry):
        v = v_ref[pl.ds(pl.multiple_of(kc * tile, tile), tile), :]
        return _softmax_step(carry, logits(kc), v)

    carry = lax.fori_loop(0, i, body, _softmax_init(tile, HEAD_DIM))
    rowi = lax.broadcasted_iota(jnp.int32, (tile, tile), 0)
    coli = lax.broadcasted_iota(jnp.int32, (tile, tile), 1)
    s = jnp.where(coli <= rowi, logits(i), NEG)
    v = v_ref[pl.ds(pl.multiple_of(i * tile, tile), tile), :]
    _, l, acc = _softmax_step(carry, s, v)
    o_ref[...] = (acc / l).astype(o_ref.dtype)


def _fox_attention(proj, c, tile):
    b, t, _ = proj.shape
    nk = t // tile
    qb, kb, vb = OFF_FQ // HEAD_DIM, OFF_FK // HEAD_DIM, OFF_FV // HEAD_DIM
    return pl.pallas_call(
        functools.partial(_fox_attn_kernel, tile=tile, scale=HEAD_DIM ** -0.5),
        out_shape=jax.ShapeDtypeStruct((b, t, FOX_W), BF16),
        grid=(b, FOX_HEADS, nk),
        in_specs=[
            pl.BlockSpec((None, tile, HEAD_DIM), lambda bi, h, i: (bi, i, qb + h)),
            pl.BlockSpec((None, t, HEAD_DIM), lambda bi, h, i: (bi, 0, kb + h)),
            pl.BlockSpec((None, t, HEAD_DIM), lambda bi, h, i: (bi, 0, vb + h)),
            pl.BlockSpec((None, None, nk, 1, tile), lambda bi, h, i: (bi, h, 0, 0, 0)),
        ],
        out_specs=pl.BlockSpec((None, tile, HEAD_DIM), lambda bi, h, i: (bi, i, h)),
        compiler_params=_params(("parallel", "parallel", "arbitrary")),
        name="fox_attention",
    )(proj, proj, proj, c.reshape(b, FOX_HEADS, nk, 1, tile))


def _rope(x, cos, sin):
    half = ROPE_DIM // 2
    lane = lax.broadcasted_iota(jnp.int32, x.shape, 1)
    swapped = jnp.where(lane < half, pltpu.roll(x, LANES - half, axis=1), pltpu.roll(x, half, axis=1))
    return x * cos + swapped * sin


def _compress(xs_ref, x, w1_ref, w2_ref, pe_ref):
    t = x.shape[0]
    nch = t // CMP_STRIDE
    xs_ref[...] = x
    pe = pe_ref[...]
    first, second = [], []
    for p in range(CMP_STRIDE):
        xp = xs_ref[pl.ds(p, nch, stride=CMP_STRIDE), :]
        first.append((xp + pe[p:p + 1]).astype(BF16))
        second.append((xp + pe[CMP_STRIDE + p:CMP_STRIDE + p + 1]).astype(BF16))
    kw = CMP_STRIDE * HEAD_DIM
    a = _dot(jnp.concatenate(first, axis=1), w1_ref[0:kw, :])
    b = _dot(jnp.concatenate(second, axis=1), w1_ref[kw:2 * kw, :])
    hidden = jax.nn.gelu(a + pltpu.roll(b, nch - 1, axis=0))
    return _dot(hidden.astype(BF16), w2_ref[...])


def _nsa_prep_kernel(ck_ref, cv_ref, sk_ref, wk_ref, cos_ref, sin_ref,
                     w1k_ref, w2k_ref, pek_ref, w1v_ref, w2v_ref, pev_ref,
                     kc_ref, vc_ref, ske_ref, wkr_ref, xs_ref):
    cos = cos_ref[...]
    sin = sin_ref[...]
    t = cos.shape[0]
    kc_ref[...] = _compress(xs_ref, _rope(ck_ref[...].astype(F32), cos, sin), w1k_ref, w2k_ref, pek_ref).astype(BF16)
    vc_ref[...] = _compress(xs_ref, cv_ref[...].astype(F32), w1v_ref, w2v_ref, pev_ref).astype(BF16)
    ske_ref[:, 0:HEAD_DIM] = _rope(sk_ref[...].astype(F32), cos, sin).astype(BF16)
    row = lax.broadcasted_iota(jnp.int32, (t, LANES), 0)
    lane = lax.broadcasted_iota(jnp.int32, (t, LANES), 1)
    ske_ref[:, HEAD_DIM:2 * HEAD_DIM] = jnp.where(row // SLC_BLOCK == lane, 1.0, 0.0).astype(BF16)
    wkr_ref[...] = _rope(wk_ref[...].astype(F32), cos, sin).astype(BF16)


def _nsa_prep(proj, cos, sin, w1k, w2k, pek, w1v, w2v, pev, off_kv):
    b, t, _ = proj.shape
    g = NSA_KV_HEADS
    nch = t // CMP_STRIDE
    blk = lambda sec: (lambda bi, gi: (bi, 0, (off_kv + sec * NSA_KV_W) // HEAD_DIM + gi))
    full = lambda shape: pl.BlockSpec(shape, lambda bi, gi: (0,) * len(shape))
    return pl.pallas_call(
        _nsa_prep_kernel,
        out_shape=(
            jax.ShapeDtypeStruct((b, g, nch, HEAD_DIM), BF16),
            jax.ShapeDtypeStruct((b, g, nch, HEAD_DIM), BF16),
            jax.ShapeDtypeStruct((b, g, t, 2 * HEAD_DIM), BF16),
            jax.ShapeDtypeStruct((b, g, t, HEAD_DIM), BF16),
        ),
        grid=(b, g),
        in_specs=[
            pl.BlockSpec((None, t, HEAD_DIM), blk(0)),
            pl.BlockSpec((None, t, HEAD_DIM), blk(1)),
            pl.BlockSpec((None, t, HEAD_DIM), blk(2)),
            pl.BlockSpec((None, t, HEAD_DIM), blk(4)),
            full((t, LANES)), full((t, LANES)),
            full(w1k.shape), full(w2k.shape), full(pek.shape),
            full(w1v.shape), full(w2v.shape), full(pev.shape),
        ],
        out_specs=(
            pl.BlockSpec((None, None, nch, HEAD_DIM), lambda bi, gi: (bi, gi, 0, 0)),
            pl.BlockSpec((None, None, nch, HEAD_DIM), lambda bi, gi: (bi, gi, 0, 0)),
            pl.BlockSpec((None, None, t, 2 * HEAD_DIM), lambda bi, gi: (bi, gi, 0, 0)),
            pl.BlockSpec((None, None, t, HEAD_DIM), lambda bi, gi: (bi, gi, 0, 0)),
        ),
        scratch_shapes=[pltpu.VMEM((t, HEAD_DIM), F32)],
        compiler_params=_params(("parallel", "parallel")),
        name="nsa_prep",
    )(proj, proj, proj, proj, cos, sin, w1k, w2k, pek, w1v, w2v, pev)


def _nsa_cmp_kernel(q_ref, cos_ref, sin_ref, kc_ref, vc_ref, qr_ref, ocmp_ref, sb_ref, *, tq, scale):
    q0 = pl.program_id(2) * tq
    cos = cos_ref[...]
    sin = sin_ref[...]
    kc = kc_ref[...]
    vc = vc_ref[...]
    nch = kc.shape[0]
    n_real = nch - 1
    nsel = (nch * CMP_STRIDE) // SLC_BLOCK
    tpos = q0 + lax.broadcasted_iota(jnp.int32, (tq, nch), 0)
    nidx = lax.broadcasted_iota(jnp.int32, (tq, nch), 1)
    valid = (nidx * CMP_STRIDE + (CMP_LEN - 1) <= tpos) & (nidx < n_real)

    psum = jnp.zeros((tq, nch), F32)
    for j in range(NSA_GROUP):
        sl = slice(j * HEAD_DIM, (j + 1) * HEAD_DIM)
        qj = _rope(q_ref[:, sl].astype(F32), cos, sin).astype(BF16)
        qr_ref[:, sl] = qj
        lg = jnp.where(valid, _dot_nt(qj, kc) * scale, NEG)
        e = jnp.where(valid, jnp.exp(lg - jnp.max(lg, axis=-1, keepdims=True)), 0.0)
        den = jnp.sum(e, axis=-1, keepdims=True)
        p = e / jnp.where(den > 0.0, den, 1.0)
        ocmp_ref[:, sl] = _dot(p.astype(BF16), vc).astype(ocmp_ref.dtype)
        psum = psum + p

    ratio = SLC_BLOCK // CMP_STRIDE
    jrow = lax.broadcasted_iota(jnp.int32, (nsel, nch), 0)
    ncol = lax.broadcasted_iota(jnp.int32, (nsel, nch), 1)
    lo_n = jrow * ratio - (CMP_LEN - 1) // CMP_STRIDE
    hi_n = jrow * ratio + (SLC_BLOCK - 1) // CMP_STRIDE
    overlap_t = jnp.where((ncol >= lo_n) & (ncol <= hi_n) & (ncol < n_real), 1.0, 0.0).astype(BF16)
    p_hi = psum.astype(BF16)
    r1 = psum - p_hi.astype(F32)
    p_mid = r1.astype(BF16)
    p_lo = (r1 - p_mid.astype(F32)).astype(BF16)
    pslc_t = _dot_nt(overlap_t, p_hi) + _dot_nt(overlap_t, p_mid) + _dot_nt(overlap_t, p_lo)

    blk = lax.broadcasted_iota(jnp.int32, (nsel, tq), 0)
    cur = (q0 + lax.broadcasted_iota(jnp.int32, (nsel, tq), 1)) // SLC_BLOCK
    forced = (blk == 0) | (blk == cur) | (blk == cur - 1)
    score = jnp.where(forced, FORCED_SCORE, jnp.where(blk <= cur, pslc_t, -1.0))

    rank = jnp.zeros((nsel, tq), F32)
    for i in range(nsel):
        si = score[i:i + 1, :]
        beats = (si > score) | ((si == score) & (blk > i))
        rank = rank + jnp.where(beats, 1.0, 0.0)
    k_top = min(SLC_TOPK, nsel)
    bias_t = jnp.where(rank < k_top, 0.0, NEG)
    bias_t = jnp.concatenate([bias_t, jnp.zeros((LANES - nsel, tq), F32)], axis=0)
    sb_ref[...] = bias_t.T.astype(BF16)


def _nsa_cmp(proj, cos, sin, kc, vc, tq):
    b, t, _ = proj.shape
    g = NSA_KV_HEADS
    gw = NSA_GROUP * HEAD_DIM
    nch = kc.shape[2]
    assert (nch * CMP_STRIDE) // SLC_BLOCK <= LANES // 2
    return pl.pallas_call(
        functools.partial(_nsa_cmp_kernel, tq=tq, scale=HEAD_DIM ** -0.5),
        out_shape=(
            jax.ShapeDtypeStruct((b, t, NSA_W), BF16),
            jax.ShapeDtypeStruct((b, t, NSA_W), F32),
            jax.ShapeDtypeStruct((b, g, t, LANES), BF16),
        ),
        grid=(b, g, t // tq),
        in_specs=[
            pl.BlockSpec((None, tq, gw), lambda bi, gi, i: (bi, i, OFF_NQ // gw + gi)),
            pl.BlockSpec((tq, LANES), lambda bi, gi, i: (i, 0)),
            pl.BlockSpec((tq, LANES), lambda bi, gi, i: (i, 0)),
            pl.BlockSpec((None, None, nch, HEAD_DIM), lambda bi, gi, i: (bi, gi, 0, 0)),
            pl.BlockSpec((None, None, nch, HEAD_DIM), lambda bi, gi, i: (bi, gi, 0, 0)),
        ],
        out_specs=(
            pl.BlockSpec((None, tq, gw), lambda bi, gi, i: (bi, i, gi)),
            pl.BlockSpec((None, tq, gw), lambda bi, gi, i: (bi, i, gi)),
            pl.BlockSpec((None, None, tq, LANES), lambda bi, gi, i: (bi, gi, i, 0)),
        ),
        compiler_params=_params(("parallel", "parallel", "parallel")),
        name="nsa_cmp_select",
    )(proj, cos, sin, kc, vc)


def _stack_heads(q_ref, extra=None):
    parts = []
    for j in range(NSA_GROUP):
        qj = q_ref[:, j * HEAD_DIM:(j + 1) * HEAD_DIM]
        parts.append(qj if extra is None else jnp.concatenate([qj, extra], axis=1))
    return jnp.concatenate(parts, axis=0)


def _unstack_heads(o_ref, x, tq):
    for j in range(NSA_GROUP):
        o_ref[:, j * HEAD_DIM:(j + 1) * HEAD_DIM] = x[j * tq:(j + 1) * tq, :].astype(o_ref.dtype)


def _nsa_slc_kernel(q_ref, sb_ref, ke_ref, v_ref, o_ref, *, tile, scale):
    i = pl.program_id(2)
    q = _stack_heads(q_ref, sb_ref[...])
    rows = NSA_GROUP * tile

    def logits(kc):
        k = ke_ref[pl.ds(pl.multiple_of(kc * tile, tile), tile), :]
        return _dot_nt(q, k) * scale

    def body(kc, carry):
        v = v_ref[pl.ds(pl.multiple_of(kc * tile, tile), tile), :]
        return _softmax_step(carry, logits(kc), v)

    carry = lax.fori_loop(0, i, body, _softmax_init(rows, HEAD_DIM))
    rowi = lax.broadcasted_iota(jnp.int32, (rows, tile), 0) % tile
    coli = lax.broadcasted_iota(jnp.int32, (rows, tile), 1)
    s = jnp.where(coli <= rowi, logits(i), NEG)
    v = v_ref[pl.ds(pl.multiple_of(i * tile, tile), tile), :]
    _, l, acc = _softmax_step(carry, s, v)
    _unstack_heads(o_ref, acc / l, tile)


def _nsa_slc(qr, sb, ske, proj, off_sv, tile):
    b, t, _ = qr.shape
    g = NSA_KV_HEADS
    gw = NSA_GROUP * HEAD_DIM
    return pl.pallas_call(
        functools.partial(_nsa_slc_kernel, tile=tile, scale=HEAD_DIM ** -0.5),
        out_shape=jax.ShapeDtypeStruct((b, t, NSA_W), F32),
        grid=(b, g, t // tile),
        in_specs=[
            pl.BlockSpec((None, tile, gw), lambda bi, gi, i: (bi, i, gi)),
            pl.BlockSpec((None, None, tile, LANES), lambda bi, gi, i: (bi, gi, i, 0)),
            pl.BlockSpec((None, None, t, 2 * HEAD_DIM), lambda bi, gi, i: (bi, gi, 0, 0)),
            pl.BlockSpec((None, t, HEAD_DIM), lambda bi, gi, i: (bi, 0, off_sv // HEAD_DIM + gi)),
        ],
        out_specs=pl.BlockSpec((None, tile, gw), lambda bi, gi, i: (bi, i, gi)),
        compiler_params=_params(("parallel", "parallel", "arbitrary")),
        name="nsa_selected",
    )(qr, sb, ske, proj)


def _nsa_win_kernel(q_ref, k_ref, v_ref, o_ref, *, tile, scale):
    i = pl.program_id(2)
    q = _stack_heads(q_ref)
    rows = NSA_GROUP * tile
    back = WINDOW // tile
    rowi = lax.broadcasted_iota(jnp.int32, (rows, tile), 0) % tile
    coli = lax.broadcasted_iota(jnp.int32, (rows, tile), 1)

    def body(kc, carry):
        start = pl.multiple_of(kc * tile, tile)
        k = k_ref[pl.ds(start, tile), :]
        v = v_ref[pl.ds(start, tile), :]
        rel = (i - kc) * tile + rowi - coli
        s = jnp.where((rel >= 0) & (rel < WINDOW), _dot_nt(q, k) * scale, NEG)
        return _softmax_step(carry, s, v)

    carry = body(i, _softmax_init(rows, HEAD_DIM))
    _, l, acc = lax.fori_loop(jnp.maximum(i - back, 0), i, body, carry)
    _unstack_heads(o_ref, acc / l, tile)


def _nsa_win(qr, wkr, proj, off_wv, tile):
    b, t, _ = qr.shape
    g = NSA_KV_HEADS
    gw = NSA_GROUP * HEAD_DIM
    assert WINDOW % tile == 0
    return pl.pallas_call(
        functools.partial(_nsa_win_kernel, tile=tile, scale=HEAD_DIM ** -0.5),
        out_shape=jax.ShapeDtypeStruct((b, t, NSA_W), F32),
        grid=(b, g, t // tile),
        in_specs=[
            pl.BlockSpec((None, tile, gw), lambda bi, gi, i: (bi, i, gi)),
            pl.BlockSpec((None, None, t, HEAD_DIM), lambda bi, gi, i: (bi, gi, 0, 0)),
            pl.BlockSpec((None, t, HEAD_DIM), lambda bi, gi, i: (bi, 0, off_wv // HEAD_DIM + gi)),
        ],
        out_specs=pl.BlockSpec((None, tile, gw), lambda bi, gi, i: (bi, i, gi)),
        compiler_params=_params(("parallel", "parallel", "arbitrary")),
        name="nsa_window",
    )(qr, wkr, proj)


def _mem_attn_kernel(q_ref, k_ref, v_ref, o_ref, *, scale):
    s = _dot_nt(q_ref[...], k_ref[...]) * scale
    e = jnp.exp(s - jnp.max(s, axis=-1, keepdims=True))
    p = e / jnp.sum(e, axis=-1, keepdims=True)
    o_ref[...] = _dot(p.astype(BF16), v_ref[...]).astype(o_ref.dtype)


def _mem_attention(proj, mkv, off_mq, tq):
    b, t, _ = proj.shape
    m = mkv.shape[1]
    d = MEM_HEAD_DIM
    return pl.pallas_call(
        functools.partial(_mem_attn_kernel, scale=d ** -0.5),
        out_shape=jax.ShapeDtypeStruct((b, t, MEM_W), BF16),
        grid=(b, MEM_HEADS, t // tq),
        in_specs=[
            pl.BlockSpec((None, tq, d), lambda bi, h, i: (bi, i, off_mq // d + h)),
            pl.BlockSpec((None, m, d), lambda bi, h, i: (bi, 0, h)),
            pl.BlockSpec((None, m, d), lambda bi, h, i: (bi, 0, MEM_HEADS + h)),
        ],
        out_specs=pl.BlockSpec((None, tq, d), lambda bi, h, i: (bi, i, h)),
        compiler_params=_params(("parallel", "parallel", "parallel")),
        name="mem_attention",
    )(proj, mkv, mkv)


def _merge_kernel(ofox_ref, ocmp_ref, oslc_ref, owin_ref, small_ref, omem_ref,
                  bg0_ref, bg1_ref, bg2_ref, wf_ref, wn_ref, wm_ref, o_ref, onsa_ref):
    @pl.when(pl.program_id(1) == 0)
    def _():
        gl = jax.nn.sigmoid(small_ref[...])
        for h in range(NSA_HEADS):
            sl = slice(h * HEAD_DIM, (h + 1) * HEAD_DIM)
            c0 = SMALL_NG + h
            c1 = SMALL_NG + NSA_HEADS + h
            c2 = SMALL_NG + 2 * NSA_HEADS + h
            onsa_ref[:, sl] = (gl[:, c0:c0 + 1] * ocmp_ref[:, sl] + gl[:, c1:c1 + 1] * oslc_ref[:, sl]
                               + gl[:, c2:c2 + 1] * owin_ref[:, sl]).astype(BF16)

    sig = lambda r: jax.nn.sigmoid(r[...].astype(F32))
    o_ref[...] = (sig(bg0_ref) * _dot(ofox_ref[...], wf_ref[...])
                  + sig(bg1_ref) * _dot(onsa_ref[...], wn_ref[...])
                  + sig(bg2_ref) * _dot(omem_ref[...], wm_ref[...])).astype(o_ref.dtype)


def _merge(ofox, ocmp, oslc, owin, small, omem, proj, wf, wn, wm, tm, tn):
    m, dm = ofox.shape[0], wf.shape[1]
    row = lambda w: pl.BlockSpec((tm, w), lambda i, j: (i, 0))
    gate = lambda br: pl.BlockSpec((tm, tn), lambda i, j: (i, (OFF_BG + br * dm) // tn + j))
    wspec = lambda w: pl.BlockSpec((w.shape[0], tn), lambda i, j: (0, j))
    return pl.pallas_call(
        _merge_kernel,
        out_shape=jax.ShapeDtypeStruct((m, dm), BF16),
        grid=(m // tm, dm // tn),
        in_specs=[row(FOX_W), row(NSA_W), row(NSA_W), row(NSA_W), row(SMALL_W), row(MEM_W),
                  gate(0), gate(1), gate(2), wspec(wf), wspec(wn), wspec(wm)],
        out_specs=pl.BlockSpec((tm, tn), lambda i, j: (i, j)),
        scratch_shapes=[pltpu.VMEM((tm, NSA_W), BF16)],
        compiler_params=_params(("parallel", "arbitrary")),
        name="branch_merge",
    )(ofox, ocmp, oslc, owin, small, omem, proj, proj, proj, wf, wn, wm)


def _out_proj_kernel(m_ref, w_ref, g_ref, h_ref, o_ref):
    o_ref[...] = h_ref[...] + _rms(_dot(m_ref[...], w_ref[...]), g_ref[...])


def _out_proj(merged, w, g, h, tm):
    m, d = h.shape
    return pl.pallas_call(
        _out_proj_kernel,
        out_shape=jax.ShapeDtypeStruct((m, d), F32),
        grid=(m // tm,),
        in_specs=[
            pl.BlockSpec((tm, d), lambda i: (i, 0)),
            pl.BlockSpec((d, d), lambda i: (0, 0)),
            pl.BlockSpec((1, d), lambda i: (0, 0)),
            pl.BlockSpec((tm, d), lambda i: (i, 0)),
        ],
        out_specs=pl.BlockSpec((tm, d), lambda i: (i, 0)),
        compiler_params=_params(("parallel",)),
        name="out_proj",
    )(merged, w, g.reshape(1, d), h)


def _mlp_kernel(h_ref, gpre_ref, w1_ref, w2_ref, gpost_ref, o_ref, u_ref, acc_ref):
    f = pl.program_id(1)

    @pl.when(f == 0)
    def _():
        u_ref[...] = _rms(h_ref[...], gpre_ref[...]).astype(BF16)
        acc_ref[...] = jnp.zeros_like(acc_ref)

    a = jnp.square(jnp.maximum(_dot(u_ref[...], w1_ref[...]), 0.0))
    acc_ref[...] += _dot(a.astype(BF16), w2_ref[...])

    @pl.when(f == pl.num_programs(1) - 1)
    def _():
        o_ref[...] = h_ref[...] + _rms(acc_ref[...], gpost_ref[...])


def _mlp(h, gpre, w1, w2, gpost, tm, tf):
    m, d = h.shape
    dff = w1.shape[1]
    return pl.pallas_call(
        _mlp_kernel,
        out_shape=jax.ShapeDtypeStruct((m, d), F32),
        grid=(m // tm, dff // tf),
        in_specs=[
            pl.BlockSpec((tm, d), lambda i, f: (i, 0)),
            pl.BlockSpec((1, d), lambda i, f: (0, 0)),
            pl.BlockSpec((d, tf), lambda i, f: (0, f)),
            pl.BlockSpec((tf, d), lambda i, f: (f, 0)),
            pl.BlockSpec((1, d), lambda i, f: (0, 0)),
        ],
        out_specs=pl.BlockSpec((tm, d), lambda i, f: (i, 0)),
        scratch_shapes=[pltpu.VMEM((tm, d), BF16), pltpu.VMEM((tm, d), F32)],
        compiler_params=_params(("parallel", "arbitrary")),
        name="mlp",
    )(h, gpre.reshape(1, d), w1, w2, gpost.reshape(1, d))


def _rope_tables(t):
    half = ROPE_DIM // 2
    inv_freq = ROPE_THETA ** (-jnp.arange(half, dtype=F32) / half)
    ang = jnp.arange(t, dtype=jnp.int32).astype(F32)[:, None] * inv_freq[None, :]
    cos, sin = jnp.cos(ang), jnp.sin(ang)
    pad = HEAD_DIM - ROPE_DIM
    cos_t = jnp.concatenate([cos, cos, jnp.ones((t, pad), F32)], axis=1)
    sin_t = jnp.concatenate([-sin, sin, jnp.zeros((t, pad), F32)], axis=1)
    return cos_t, sin_t


def _split_w_in(w, d_model):
    widths = (FOX_W, FOX_W, FOX_W, FOX_HEADS, NSA_W) + (NSA_KV_W,) * 6 + (3 * NSA_HEADS, MEM_W, N_BRANCHES * d_model)
    offs = np.concatenate([[0], np.cumsum(widths)])
    seg = lambda i: w[:, offs[i]:offs[i + 1]]
    main = jnp.concatenate([seg(0), seg(1), seg(2), seg(4), seg(13)] + [seg(i) for i in range(5, 11)] + [seg(12)],
                           axis=1).astype(BF16)
    small = jnp.concatenate([seg(3), seg(11), jnp.zeros((w.shape[0], SMALL_W - FOX_HEADS - 3 * NSA_HEADS), w.dtype)],
                            axis=1).astype(BF16)
    return main, small


def _pick(n, candidates):
    for c in candidates:
        if n % c == 0:
            return c
    raise ValueError(f"no tile for {n}")


def kernel(x, mem, w_in, b_f, w_cmp1_k, w_cmp2_k, pe_cmp_k, w_cmp1_v, w_cmp2_v, pe_cmp_v, w_mem_kv, g_mem, w_up_fox, w_up_nsa, w_up_mem, w_o, g_pre_mix, g_post_mix, g_pre_mlp, g_post_mlp, w_mlp1, w_mlp2):
    b, t, d = x.shape
    depth = w_in.shape[0]
    n = b * t
    mlen = mem.shape[1]
    off_kv = OFF_BG + N_BRANCHES * d
    off_mq = off_kv + 6 * NSA_KV_W
    n_main = off_mq + MEM_W

    cos_t, sin_t = _rope_tables(t)
    tm_proj = _pick(n, (1024, 512, 256))
    tn_proj = _pick(n_main, (1280, 1024, 512, 256, 128))
    attn_tile = _pick(t, (512, 256, 128))
    nsa_tile = _pick(t, (256, 128))
    tm_mem = _pick(b * mlen, (1024, 512, 256))

    h = x.reshape(n, d)
    mem2 = mem.reshape(b * mlen, d)
    zero_small = jnp.zeros((d, SMALL_W), BF16)
    for l in range(depth):
        w_main, w_small = _split_w_in(w_in[l], d)
        proj, small = _norm_matmul(h, g_pre_mix[l], w_main, w_small, tm_proj, tn_proj, "in_proj")
        proj = proj.reshape(b, t, n_main)

        ff = small[:, SMALL_FF:SMALL_FF + FOX_HEADS].reshape(b, t, FOX_HEADS).transpose(0, 2, 1)
        c = _fox_cumsum(ff, b_f[l])
        o_fox = _fox_attention(proj, c, attn_tile)

        kc, vc, ske, wkr = _nsa_prep(proj, cos_t, sin_t,
                                     w_cmp1_k[l].astype(BF16), w_cmp2_k[l].astype(BF16), pe_cmp_k[l],
                                     w_cmp1_v[l].astype(BF16), w_cmp2_v[l].astype(BF16), pe_cmp_v[l], off_kv)
        qr, o_cmp, sb = _nsa_cmp(proj, cos_t, sin_t, kc, vc, nsa_tile)
        o_slc = _nsa_slc(qr, sb, ske, proj, off_kv + 3 * NSA_KV_W, nsa_tile)
        o_win = _nsa_win(qr, wkr, proj, off_kv + 5 * NSA_KV_W, nsa_tile)

        mkv, _ = _norm_matmul(mem2, g_mem[l], w_mem_kv[l].astype(BF16), zero_small, tm_mem, 1024, "mem_kv")
        o_mem = _mem_attention(proj, mkv.reshape(b, mlen, 2 * MEM_W), off_mq, attn_tile)

        merged = _merge(o_fox.reshape(n, FOX_W), o_cmp.reshape(n, NSA_W), o_slc.reshape(n, NSA_W),
                        o_win.reshape(n, NSA_W), small, o_mem.reshape(n, MEM_W), proj.reshape(n, n_main),
                        w_up_fox[l].astype(BF16), w_up_nsa[l].astype(BF16), w_up_mem[l].astype(BF16), 512, 1024)
        h = _out_proj(merged, w_o[l].astype(BF16), g_post_mix[l], h, 512)
        h = _mlp(h, g_pre_mlp[l], w_mlp1[l].astype(BF16), w_mlp2[l].astype(BF16), g_post_mlp[l], 512, 1024)
    return h.reshape(b, t, d)
```

```python
import functools

import numpy as np
import jax
import jax.numpy as jnp
from jax import lax
from jax.experimental import pallas as pl
from jax.experimental.pallas import tpu as pltpu

F32 = jnp.float32
BF16 = jnp.bfloat16

HEAD_DIM = 128
FOX_HEADS = 8
NSA_HEADS = 8
NSA_KV_HEADS = 2
NSA_GROUP = NSA_HEADS // NSA_KV_HEADS
MEM_HEADS = 4
MEM_HEAD_DIM = 256
N_BRANCHES = 3
CMP_LEN = 32
CMP_STRIDE = 16
SLC_BLOCK = 64
SLC_TOPK = 16
WINDOW = 512
ROPE_THETA = 500000.0
ROPE_DIM = HEAD_DIM // 4
RMS_EPS = 1e-6
NEG = -1e30
FORCED_SCORE = 1e6
LOG2E = 1.4426950408889634

LANES = 128
VMEM_LIMIT = 56 * 1024 * 1024

FOX_W = FOX_HEADS * HEAD_DIM
NSA_W = NSA_HEADS * HEAD_DIM
NSA_KV_W = NSA_KV_HEADS * HEAD_DIM
MEM_W = MEM_HEADS * MEM_HEAD_DIM
OFF_FQ = 0
OFF_FK = OFF_FQ + FOX_W
OFF_FV = OFF_FK + FOX_W
OFF_NQ = OFF_FV + FOX_W
OFF_BG = OFF_NQ + NSA_W
SMALL_W = LANES
SMALL_FF = 0
SMALL_NG = FOX_HEADS


def _params(sem):
    return pltpu.CompilerParams(dimension_semantics=sem, vmem_limit_bytes=VMEM_LIMIT)


def _rms(x, g):
    return (x * lax.rsqrt(jnp.mean(x * x, axis=-1, keepdims=True) + RMS_EPS)) * g


def _dot(a, b):
    return jnp.dot(a, b, preferred_element_type=F32)


def _dot_nt(a, b):
    return lax.dot_general(a, b, (((1,), (1,)), ((), ())), preferred_element_type=F32)


def _norm_matmul_kernel(x_ref, g_ref, w_ref, ws_ref, o_ref, os_ref, u_ref):
    @pl.when(pl.program_id(1) == 0)
    def _():
        u_ref[...] = _rms(x_ref[...], g_ref[...]).astype(BF16)
        os_ref[...] = _dot(u_ref[...], ws_ref[...])

    o_ref[...] = _dot(u_ref[...], w_ref[...]).astype(o_ref.dtype)


def _norm_matmul(x, g, w, ws, tm, tn, name):
    m, d = x.shape
    n = w.shape[1]
    return pl.pallas_call(
        _norm_matmul_kernel,
        out_shape=(jax.ShapeDtypeStruct((m, n), BF16), jax.ShapeDtypeStruct((m, SMALL_W), F32)),
        grid=(m // tm, n // tn),
        in_specs=[
            pl.BlockSpec((tm, d), lambda i, j: (i, 0)),
            pl.BlockSpec((1, d), lambda i, j: (0, 0)),
            pl.BlockSpec((d, tn), lambda i, j: (0, j)),
            pl.BlockSpec((d, SMALL_W), lambda i, j: (0, 0)),
        ],
        out_specs=(
            pl.BlockSpec((tm, tn), lambda i, j: (i, j)),
            pl.BlockSpec((tm, SMALL_W), lambda i, j: (i, 0)),
        ),
        scratch_shapes=[pltpu.VMEM((tm, d), BF16)],
        compiler_params=_params(("parallel", "arbitrary")),
        name=name,
    )(x, g.reshape(1, d), w, ws)


def _fox_cumsum_kernel(bf_ref, ff_ref, c_ref):
    x = ff_ref[...] + bf_ref[pl.program_id(1)]
    lf = jnp.minimum(x, 0.0) - jnp.log1p(jnp.exp(-jnp.abs(x)))
    rows, lanes = lf.shape
    lane = lax.broadcasted_iota(jnp.int32, lf.shape, 1)
    row = lax.broadcasted_iota(jnp.int32, lf.shape, 0)
    y = lf
    s = 1
    while s < lanes:
        y = y + jnp.where(lane >= s, pltpu.roll(y, s, axis=1), 0.0)
        s *= 2
    tot = jnp.broadcast_to(y[:, lanes - 1:lanes], lf.shape)
    z = tot
    s = 1
    while s < rows:
        z = z + jnp.where(row >= s, pltpu.roll(z, s, axis=0), 0.0)
        s *= 2
    c_ref[...] = (y + (z - tot)) * LOG2E


def _fox_cumsum(ff, b_f):
    b, h, t = ff.shape
    rows = t // LANES
    out = pl.pallas_call(
        _fox_cumsum_kernel,
        out_shape=jax.ShapeDtypeStruct((b, h, rows, LANES), F32),
        grid=(b, h),
        in_specs=[
            pl.BlockSpec(memory_space=pltpu.SMEM),
            pl.BlockSpec((None, None, rows, LANES), lambda i, j: (i, j, 0, 0)),
        ],
        out_specs=pl.BlockSpec((None, None, rows, LANES), lambda i, j: (i, j, 0, 0)),
        compiler_params=_params(("parallel", "parallel")),
        name="fox_cumsum",
    )(b_f, ff.reshape(b, h, rows, LANES))
    return out.reshape(b, h, t)


def _softmax_step(carry, s, v):
    m, l, acc = carry
    m_new = jnp.maximum(m, jnp.max(s, axis=-1, keepdims=True))
    alpha = jnp.exp2(m - m_new)
    p = jnp.exp2(s - m_new)
    l = alpha * l + jnp.sum(p, axis=-1, keepdims=True)
    acc = alpha * acc + _dot(p.astype(BF16), v)
    return m_new, l, acc


def _softmax_init(rows, d):
    return (jnp.full((rows, 1), -jnp.inf, F32), jnp.zeros((rows, 1), F32), jnp.zeros((rows, d), F32))


def _scale_q(q, qscale):
    return (q.astype(F32) * qscale).astype(BF16)


def _fox_attn_kernel(q_ref, k_ref, v_ref, c_ref, o_ref, *, tq, tk, groups, qscale):
    i = pl.program_id(2)
    rq = tq // groups
    qs = [_scale_q(q_ref[gi * rq:(gi + 1) * rq, :], qscale) for gi in range(groups)]
    n_full = (i * tq) // tk
    diag_off = i * tq - n_full * tk

    def logits(kc):
        k = k_ref[pl.ds(pl.multiple_of(kc * tk, tk), tk), :]
        cb = c_ref[kc]
        return tuple(_dot_nt(qs[gi], k) - cb for gi in range(groups))

    def values(kc):
        return v_ref[pl.ds(pl.multiple_of(kc * tk, tk), tk), :]

    def body(kc, states):
        s = logits(kc)
        v = values(kc)
        return tuple(_softmax_step(states[gi], s[gi], v) for gi in range(groups))

    init = tuple(_softmax_init(rq, HEAD_DIM) for _ in range(groups))
    states = lax.fori_loop(0, n_full, body, init)
    s = logits(n_full)
    v = values(n_full)
    for gi in range(groups):
        rowi = lax.broadcasted_iota(jnp.int32, (rq, tk), 0) + (diag_off + gi * rq)
        coli = lax.broadcasted_iota(jnp.int32, (rq, tk), 1)
        _, l, acc = _softmax_step(states[gi], jnp.where(coli <= rowi, s[gi], NEG), v)
        o_ref[gi * rq:(gi + 1) * rq, :] = (acc / l).astype(o_ref.dtype)


def _fox_attention(proj, c2, tq, tk, groups):
    b, t, _ = proj.shape
    nk = t // tk
    assert tk % tq == 0
    qb, kb, vb = OFF_FQ // HEAD_DIM, OFF_FK // HEAD_DIM, OFF_FV // HEAD_DIM
    return pl.pallas_call(
        functools.partial(_fox_attn_kernel, tq=tq, tk=tk, groups=groups, qscale=HEAD_DIM ** -0.5 * LOG2E),
        out_shape=jax.ShapeDtypeStruct((b, t, FOX_W), BF16),
        grid=(b, FOX_HEADS, t // tq),
        in_specs=[
            pl.BlockSpec((None, tq, HEAD_DIM), lambda bi, h, i: (bi, i, qb + h)),
            pl.BlockSpec((None, t, HEAD_DIM), lambda bi, h, i: (bi, 0, kb + h)),
            pl.BlockSpec((None, t, HEAD_DIM), lambda bi, h, i: (bi, 0, vb + h)),
            pl.BlockSpec((None, None, nk, 1, tk), lambda bi, h, i: (bi, h, 0, 0, 0)),
        ],
        out_specs=pl.BlockSpec((None, tq, HEAD_DIM), lambda bi, h, i: (bi, i, h)),
        compiler_params=_params(("parallel", "parallel", "arbitrary")),
        name="fox_attention",
    )(proj, proj, proj, c2.reshape(b, FOX_HEADS, nk, 1, tk))


def _rope(x, cos, sin):
    half = ROPE_DIM // 2
    lane = lax.broadcasted_iota(jnp.int32, x.shape, 1)
    swapped = jnp.where(lane < half, pltpu.roll(x, LANES - half, axis=1), pltpu.roll(x, half, axis=1))
    return x * cos + swapped * sin


def _compress(xs_ref, x, w1_ref, w2_ref, pe_ref):
    t = x.shape[0]
    nch = t // CMP_STRIDE
    xs_ref[...] = x
    pe = pe_ref[...]
    first, second = [], []
    for p in range(CMP_STRIDE):
        xp = xs_ref[pl.ds(p, nch, stride=CMP_STRIDE), :]
        first.append((xp + pe[p:p + 1]).astype(BF16))
        second.append((xp + pe[CMP_STRIDE + p:CMP_STRIDE + p + 1]).astype(BF16))
    kw = CMP_STRIDE * HEAD_DIM
    a = _dot(jnp.concatenate(first, axis=1), w1_ref[0:kw, :])
    b = _dot(jnp.concatenate(second, axis=1), w1_ref[kw:2 * kw, :])
    hidden = jax.nn.gelu(a + pltpu.roll(b, nch - 1, axis=0))
    return _dot(hidden.astype(BF16), w2_ref[...])


def _store_transposed_chunks(dst_ref, src_ref):
    nk, _, tile = dst_ref.shape
    for c in range(nk):
        dst_ref[c] = src_ref[c * tile:(c + 1) * tile, :].astype(F32).T.astype(dst_ref.dtype)


def _nsa_prep_kernel(ck_ref, cv_ref, sk_ref, sv_ref, wk_ref, wv_ref, cos_ref, sin_ref,
                     w1k_ref, w2k_ref, pek_ref, w1v_ref, w2v_ref, pev_ref,
                     kc_ref, vc_ref, ske_ref, wkr_ref, svt_ref, wvt_ref, xs_ref):
    cos = cos_ref[...]
    sin = sin_ref[...]
    t = cos.shape[0]
    _store_transposed_chunks(svt_ref, sv_ref)
    _store_transposed_chunks(wvt_ref, wv_ref)
    kc_ref[...] = _compress(xs_ref, _rope(ck_ref[...].astype(F32), cos, sin), w1k_ref, w2k_ref, pek_ref).astype(BF16)
    vc_ref[...] = _compress(xs_ref, cv_ref[...].astype(F32), w1v_ref, w2v_ref, pev_ref).astype(BF16)
    ske_ref[:, 0:HEAD_DIM] = _rope(sk_ref[...].astype(F32), cos, sin).astype(BF16)
    row = lax.broadcasted_iota(jnp.int32, (t, LANES), 0)
    lane = lax.broadcasted_iota(jnp.int32, (t, LANES), 1)
    ske_ref[:, HEAD_DIM:2 * HEAD_DIM] = jnp.where(row // SLC_BLOCK == lane, 1.0, 0.0).astype(BF16)
    wkr_ref[...] = _rope(wk_ref[...].astype(F32), cos, sin).astype(BF16)


def _nsa_prep(proj, cos, sin, w1k, w2k, pek, w1v, w2v, pev, off_kv, slc_tk, win_tk):
    b, t, _ = proj.shape
    g = NSA_KV_HEADS
    nch = t // CMP_STRIDE
    blk = lambda sec: (lambda bi, gi: (bi, 0, (off_kv + sec * NSA_KV_W) // HEAD_DIM + gi))
    full = lambda shape: pl.BlockSpec(shape, lambda bi, gi: (0,) * len(shape))
    per_head = lambda shape: pl.BlockSpec((None, None) + shape, lambda bi, gi: (bi, gi) + (0,) * len(shape))
    return pl.pallas_call(
        _nsa_prep_kernel,
        out_shape=(
            jax.ShapeDtypeStruct((b, g, nch, HEAD_DIM), BF16),
            jax.ShapeDtypeStruct((b, g, nch, HEAD_DIM), BF16),
            jax.ShapeDtypeStruct((b, g, t, 2 * HEAD_DIM), BF16),
            jax.ShapeDtypeStruct((b, g, t, HEAD_DIM), BF16),
            jax.ShapeDtypeStruct((b, g, t // slc_tk, HEAD_DIM, slc_tk), BF16),
            jax.ShapeDtypeStruct((b, g, t // win_tk, HEAD_DIM, win_tk), BF16),
        ),
        grid=(b, g),
        in_specs=[pl.BlockSpec((None, t, HEAD_DIM), blk(sec)) for sec in range(6)] + [
            full((t, LANES)), full((t, LANES)),
            full(w1k.shape), full(w2k.shape), full(pek.shape),
            full(w1v.shape), full(w2v.shape), full(pev.shape),
        ],
        out_specs=(
            per_head((nch, HEAD_DIM)), per_head((nch, HEAD_DIM)),
            per_head((t, 2 * HEAD_DIM)), per_head((t, HEAD_DIM)),
            per_head((t // slc_tk, HEAD_DIM, slc_tk)), per_head((t // win_tk, HEAD_DIM, win_tk)),
        ),
        scratch_shapes=[pltpu.VMEM((t, HEAD_DIM), F32)],
        compiler_params=_params(("parallel", "parallel")),
        name="nsa_prep",
    )(proj, proj, proj, proj, proj, proj, cos, sin, w1k, w2k, pek, w1v, w2v, pev)


def _nsa_cmp_kernel(q_ref, cos_ref, sin_ref, kc_ref, vc_ref, qt_ref, ocmp_ref, sbt_ref, *, tq, scale):
    q0 = pl.program_id(2) * tq
    cos = cos_ref[...]
    sin = sin_ref[...]
    kc = kc_ref[...]
    vc = vc_ref[...]
    nch = kc.shape[0]
    n_real = nch - 1
    nsel = (nch * CMP_STRIDE) // SLC_BLOCK
    tpos = q0 + lax.broadcasted_iota(jnp.int32, (tq, nch), 0)
    nidx = lax.broadcasted_iota(jnp.int32, (tq, nch), 1)
    valid = (nidx * CMP_STRIDE + (CMP_LEN - 1) <= tpos) & (nidx < n_real)

    psum = jnp.zeros((tq, nch), F32)
    for j in range(NSA_GROUP):
        sl = slice(j * HEAD_DIM, (j + 1) * HEAD_DIM)
        qf = _rope(q_ref[:, sl].astype(F32), cos, sin)
        qt_ref[j] = (qf * (scale * LOG2E)).T.astype(BF16)
        qj = qf.astype(BF16)
        lg = jnp.where(valid, _dot_nt(qj, kc) * scale, NEG)
        e = jnp.where(valid, jnp.exp(lg - jnp.max(lg, axis=-1, keepdims=True)), 0.0)
        den = jnp.sum(e, axis=-1, keepdims=True)
        p = e / jnp.where(den > 0.0, den, 1.0)
        ocmp_ref[:, sl] = _dot(p.astype(BF16), vc).astype(ocmp_ref.dtype)
        psum = psum + p

    ratio = SLC_BLOCK // CMP_STRIDE
    jrow = lax.broadcasted_iota(jnp.int32, (nsel, nch), 0)
    ncol = lax.broadcasted_iota(jnp.int32, (nsel, nch), 1)
    lo_n = jrow * ratio - (CMP_LEN - 1) // CMP_STRIDE
    hi_n = jrow * ratio + (SLC_BLOCK - 1) // CMP_STRIDE
    overlap_t = jnp.where((ncol >= lo_n) & (ncol <= hi_n) & (ncol < n_real), 1.0, 0.0).astype(BF16)
    p_hi = psum.astype(BF16)
    r1 = psum - p_hi.astype(F32)
    p_mid = r1.astype(BF16)
    p_lo = (r1 - p_mid.astype(F32)).astype(BF16)
    pslc_t = _dot_nt(overlap_t, p_hi) + _dot_nt(overlap_t, p_mid) + _dot_nt(overlap_t, p_lo)

    blk = lax.broadcasted_iota(jnp.int32, (nsel, tq), 0)
    cur = (q0 + lax.broadcasted_iota(jnp.int32, (nsel, tq), 1)) // SLC_BLOCK
    forced = (blk == 0) | (blk == cur) | (blk == cur - 1)
    score = jnp.where(forced, FORCED_SCORE, jnp.where(blk <= cur, pslc_t, -1.0))

    rank = jnp.zeros((nsel, tq), F32)
    for i in range(nsel):
        si = score[i:i + 1, :]
        beats = (si > score) | ((si == score) & (blk > i))
        rank = rank + jnp.where(beats, 1.0, 0.0)
    k_top = min(SLC_TOPK, nsel)
    bias_t = jnp.where(rank < k_top, 0.0, NEG)
    sbt_ref[...] = jnp.concatenate([bias_t, jnp.zeros((LANES - nsel, tq), F32)], axis=0).astype(BF16)


def _nsa_cmp(proj, cos, sin, kc, vc, tq):
    b, t, _ = proj.shape
    g = NSA_KV_HEADS
    gw = NSA_GROUP * HEAD_DIM
    nch = kc.shape[2]
    assert (nch * CMP_STRIDE) // SLC_BLOCK <= LANES // 2
    return pl.pallas_call(
        functools.partial(_nsa_cmp_kernel, tq=tq, scale=HEAD_DIM ** -0.5),
        out_shape=(
            jax.ShapeDtypeStruct((b, g, NSA_GROUP, HEAD_DIM, t), BF16),
            jax.ShapeDtypeStruct((b, t, NSA_W), F32),
            jax.ShapeDtypeStruct((b, g, LANES, t), BF16),
        ),
        grid=(b, g, t // tq),
        in_specs=[
            pl.BlockSpec((None, tq, gw), lambda bi, gi, i: (bi, i, OFF_NQ // gw + gi)),
            pl.BlockSpec((tq, LANES), lambda bi, gi, i: (i, 0)),
            pl.BlockSpec((tq, LANES), lambda bi, gi, i: (i, 0)),
            pl.BlockSpec((None, None, nch, HEAD_DIM), lambda bi, gi, i: (bi, gi, 0, 0)),
            pl.BlockSpec((None, None, nch, HEAD_DIM), lambda bi, gi, i: (bi, gi, 0, 0)),
        ],
        out_specs=(
            pl.BlockSpec((None, None, NSA_GROUP, HEAD_DIM, tq), lambda bi, gi, i: (bi, gi, 0, 0, i)),
            pl.BlockSpec((None, tq, gw), lambda bi, gi, i: (bi, i, gi)),
            pl.BlockSpec((None, None, LANES, tq), lambda bi, gi, i: (bi, gi, 0, i)),
        ),
        compiler_params=_params(("parallel", "parallel", "parallel")),
        name="nsa_cmp_select",
    )(proj, cos, sin, kc, vc)


def _flash_t_step(carry, s, v_t):
    m, l, acc = carry
    m_new = jnp.maximum(m, jnp.max(s, axis=0, keepdims=True))
    alpha = jnp.exp2(m - m_new)
    p = jnp.exp2(s - m_new)
    l = alpha * l + jnp.sum(p, axis=0, keepdims=True)
    acc = alpha * acc + _dot(v_t, p.astype(BF16))
    return m_new, l, acc


def _flash_t_init(cols, d):
    return (jnp.full((1, cols), -jnp.inf, F32), jnp.zeros((1, cols), F32), jnp.zeros((d, cols), F32))


def _heads_on_lanes(qt_ref, extra=None):
    parts = []
    for j in range(NSA_GROUP):
        parts.append(qt_ref[j] if extra is None else jnp.concatenate([qt_ref[j], extra], axis=0))
    return jnp.concatenate(parts, axis=1)


def _store_heads(o_ref, o_t, tile):
    for j in range(NSA_GROUP):
        o_ref[:, j * HEAD_DIM:(j + 1) * HEAD_DIM] = o_t[:, j * tile:(j + 1) * tile].T.astype(o_ref.dtype)


def _nsa_slc_kernel(qt_ref, sbt_ref, ke_ref, vt_ref, o_ref, *, tq, tk):
    i = pl.program_id(2)
    q_t = _heads_on_lanes(qt_ref, sbt_ref[...])
    cols = NSA_GROUP * tq
    n_full = (i * tq) // tk

    def logits(kc):
        k = ke_ref[pl.ds(pl.multiple_of(kc * tk, tk), tk), :]
        return _dot(k, q_t)

    state = lax.fori_loop(0, n_full, lambda kc, c: _flash_t_step(c, logits(kc), vt_ref[kc]),
                          _flash_t_init(cols, HEAD_DIM))
    key = lax.broadcasted_iota(jnp.int32, (tk, cols), 0) + n_full * tk
    qry = lax.broadcasted_iota(jnp.int32, (tk, cols), 1) % tq + i * tq
    for d in range(max(tq // tk, 1)):
        s = jnp.where(key + d * tk <= qry, logits(n_full + d), NEG)
        state = _flash_t_step(state, s, vt_ref[n_full + d])
    _, l, acc = state
    _store_heads(o_ref, acc / l, tq)


def _nsa_slc(qt, sbt, ske, svt, tq):
    b, g, _, _, t = qt.shape
    gw = NSA_GROUP * HEAD_DIM
    nk, tk = svt.shape[2], svt.shape[4]
    assert tq % tk == 0 or tk % tq == 0
    return pl.pallas_call(
        functools.partial(_nsa_slc_kernel, tq=tq, tk=tk),
        out_shape=jax.ShapeDtypeStruct((b, t, NSA_W), F32),
        grid=(b, g, t // tq),
        in_specs=[
            pl.BlockSpec((None, None, NSA_GROUP, HEAD_DIM, tq), lambda bi, gi, i: (bi, gi, 0, 0, i)),
            pl.BlockSpec((None, None, LANES, tq), lambda bi, gi, i: (bi, gi, 0, i)),
            pl.BlockSpec((None, None, t, 2 * HEAD_DIM), lambda bi, gi, i: (bi, gi, 0, 0)),
            pl.BlockSpec((None, None, nk, HEAD_DIM, tk), lambda bi, gi, i: (bi, gi, 0, 0, 0)),
        ],
        out_specs=pl.BlockSpec((None, tq, gw), lambda bi, gi, i: (bi, i, gi)),
        compiler_params=_params(("parallel", "parallel", "arbitrary")),
        name="nsa_selected",
    )(qt, sbt, ske, svt)


def _nsa_win_kernel(qt_ref, k_ref, vt_ref, o_ref, *, tile):
    i = pl.program_id(2)
    q_t = _heads_on_lanes(qt_ref)
    cols = NSA_GROUP * tile
    back = WINDOW // tile
    key = lax.broadcasted_iota(jnp.int32, (tile, cols), 0)
    qry = lax.broadcasted_iota(jnp.int32, (tile, cols), 1) % tile

    def body(kc, carry):
        k = k_ref[pl.ds(pl.multiple_of(kc * tile, tile), tile), :]
        rel = (i - kc) * tile + qry - key
        s = jnp.where((rel >= 0) & (rel < WINDOW), _dot(k, q_t), NEG)
        return _flash_t_step(carry, s, vt_ref[kc])

    carry = body(i, _flash_t_init(cols, HEAD_DIM))
    _, l, acc = lax.fori_loop(jnp.maximum(i - back, 0), i, body, carry)
    _store_heads(o_ref, acc / l, tile)


def _nsa_win(qt, wkr, wvt, tile):
    b, g, _, _, t = qt.shape
    gw = NSA_GROUP * HEAD_DIM
    nk = t // tile
    assert WINDOW % tile == 0
    return pl.pallas_call(
        functools.partial(_nsa_win_kernel, tile=tile),
        out_shape=jax.ShapeDtypeStruct((b, t, NSA_W), F32),
        grid=(b, g, nk),
        in_specs=[
            pl.BlockSpec((None, None, NSA_GROUP, HEAD_DIM, tile), lambda bi, gi, i: (bi, gi, 0, 0, i)),
            pl.BlockSpec((None, None, t, HEAD_DIM), lambda bi, gi, i: (bi, gi, 0, 0)),
            pl.BlockSpec((None, None, nk, HEAD_DIM, tile), lambda bi, gi, i: (bi, gi, 0, 0, 0)),
        ],
        out_specs=pl.BlockSpec((None, tile, gw), lambda bi, gi, i: (bi, i, gi)),
        compiler_params=_params(("parallel", "parallel", "arbitrary")),
        name="nsa_window",
    )(qt, wkr, wvt)


def _mem_attn_kernel(q_ref, k_ref, v_ref, o_ref, *, scale):
    s = _dot_nt(q_ref[...], k_ref[...]) * scale
    e = jnp.exp(s - jnp.max(s, axis=-1, keepdims=True))
    p = e / jnp.sum(e, axis=-1, keepdims=True)
    o_ref[...] = _dot(p.astype(BF16), v_ref[...]).astype(o_ref.dtype)


def _mem_attention(proj, mkv, off_mq, tq):
    b, t, _ = proj.shape
    m = mkv.shape[1]
    d = MEM_HEAD_DIM
    return pl.pallas_call(
        functools.partial(_mem_attn_kernel, scale=d ** -0.5),
        out_shape=jax.ShapeDtypeStruct((b, t, MEM_W), BF16),
        grid=(b, MEM_HEADS, t // tq),
        in_specs=[
            pl.BlockSpec((None, tq, d), lambda bi, h, i: (bi, i, off_mq // d + h)),
            pl.BlockSpec((None, m, d), lambda bi, h, i: (bi, 0, h)),
            pl.BlockSpec((None, m, d), lambda bi, h, i: (bi, 0, MEM_HEADS + h)),
        ],
        out_specs=pl.BlockSpec((None, tq, d), lambda bi, h, i: (bi, i, h)),
        compiler_params=_params(("parallel", "parallel", "parallel")),
        name="mem_attention",
    )(proj, mkv, mkv)


def _merge_kernel(ofox_ref, ocmp_ref, oslc_ref, owin_ref, small_ref, omem_ref,
                  bg0_ref, bg1_ref, bg2_ref, wf_ref, wn_ref, wm_ref, o_ref, onsa_ref):
    @pl.when(pl.program_id(1) == 0)
    def _():
        gl = jax.nn.sigmoid(small_ref[...])
        for h in range(NSA_HEADS):
            sl = slice(h * HEAD_DIM, (h + 1) * HEAD_DIM)
            c0 = SMALL_NG + h
            c1 = SMALL_NG + NSA_HEADS + h
            c2 = SMALL_NG + 2 * NSA_HEADS + h
            onsa_ref[:, sl] = (gl[:, c0:c0 + 1] * ocmp_ref[:, sl] + gl[:, c1:c1 + 1] * oslc_ref[:, sl]
                               + gl[:, c2:c2 + 1] * owin_ref[:, sl]).astype(BF16)

    sig = lambda r: jax.nn.sigmoid(r[...].astype(F32))
    o_ref[...] = (sig(bg0_ref) * _dot(ofox_ref[...], wf_ref[...])
                  + sig(bg1_ref) * _dot(onsa_ref[...], wn_ref[...])
                  + sig(bg2_ref) * _dot(omem_ref[...], wm_ref[...])).astype(o_ref.dtype)


def _merge(ofox, ocmp, oslc, owin, small, omem, proj, wf, wn, wm, tm, tn):
    m, dm = ofox.shape[0], wf.shape[1]
    row = lambda w: pl.BlockSpec((tm, w), lambda i, j: (i, 0))
    gate = lambda br: pl.BlockSpec((tm, tn), lambda i, j: (i, (OFF_BG + br * dm) // tn + j))
    wspec = lambda w: pl.BlockSpec((w.shape[0], tn), lambda i, j: (0, j))
    return pl.pallas_call(
        _merge_kernel,
        out_shape=jax.ShapeDtypeStruct((m, dm), BF16),
        grid=(m // tm, dm // tn),
        in_specs=[row(FOX_W), row(NSA_W), row(NSA_W), row(NSA_W), row(SMALL_W), row(MEM_W),
                  gate(0), gate(1), gate(2), wspec(wf), wspec(wn), wspec(wm)],
        out_specs=pl.BlockSpec((tm, tn), lambda i, j: (i, j)),
        scratch_shapes=[pltpu.VMEM((tm, NSA_W), BF16)],
        compiler_params=_params(("parallel", "arbitrary")),
        name="branch_merge",
    )(ofox, ocmp, oslc, owin, small, omem, proj, proj, proj, wf, wn, wm)


def _out_proj_kernel(m_ref, w_ref, g_ref, h_ref, o_ref):
    o_ref[...] = h_ref[...] + _rms(_dot(m_ref[...], w_ref[...]), g_ref[...])


def _out_proj(merged, w, g, h, tm):
    m, d = h.shape
    return pl.pallas_call(
        _out_proj_kernel,
        out_shape=jax.ShapeDtypeStruct((m, d), F32),
        grid=(m // tm,),
        in_specs=[
            pl.BlockSpec((tm, d), lambda i: (i, 0)),
            pl.BlockSpec((d, d), lambda i: (0, 0)),
            pl.BlockSpec((1, d), lambda i: (0, 0)),
            pl.BlockSpec((tm, d), lambda i: (i, 0)),
        ],
        out_specs=pl.BlockSpec((tm, d), lambda i: (i, 0)),
        compiler_params=_params(("parallel",)),
        name="out_proj",
    )(merged, w, g.reshape(1, d), h)


def _mlp_kernel(h_ref, gpre_ref, w1_ref, w2_ref, gpost_ref, o_ref, u_ref, acc_ref):
    f = pl.program_id(1)

    @pl.when(f == 0)
    def _():
        u_ref[...] = _rms(h_ref[...], gpre_ref[...]).astype(BF16)
        acc_ref[...] = jnp.zeros_like(acc_ref)

    a = jnp.square(jnp.maximum(_dot(u_ref[...], w1_ref[...]), 0.0))
    acc_ref[...] += _dot(a.astype(BF16), w2_ref[...])

    @pl.when(f == pl.num_programs(1) - 1)
    def _():
        o_ref[...] = h_ref[...] + _rms(acc_ref[...], gpost_ref[...])


def _mlp(h, gpre, w1, w2, gpost, tm, tf):
    m, d = h.shape
    dff = w1.shape[1]
    return pl.pallas_call(
        _mlp_kernel,
        out_shape=jax.ShapeDtypeStruct((m, d), F32),
        grid=(m // tm, dff // tf),
        in_specs=[
            pl.BlockSpec((tm, d), lambda i, f: (i, 0)),
            pl.BlockSpec((1, d), lambda i, f: (0, 0)),
            pl.BlockSpec((d, tf), lambda i, f: (0, f)),
            pl.BlockSpec((tf, d), lambda i, f: (f, 0)),
            pl.BlockSpec((1, d), lambda i, f: (0, 0)),
        ],
        out_specs=pl.BlockSpec((tm, d), lambda i, f: (i, 0)),
        scratch_shapes=[pltpu.VMEM((tm, d), BF16), pltpu.VMEM((tm, d), F32)],
        compiler_params=_params(("parallel", "arbitrary")),
        name="mlp",
    )(h, gpre.reshape(1, d), w1, w2, gpost.reshape(1, d))


def _rope_tables(t):
    half = ROPE_DIM // 2
    inv_freq = ROPE_THETA ** (-jnp.arange(half, dtype=F32) / half)
    ang = jnp.arange(t, dtype=jnp.int32).astype(F32)[:, None] * inv_freq[None, :]
    cos, sin = jnp.cos(ang), jnp.sin(ang)
    pad = HEAD_DIM - ROPE_DIM
    cos_t = jnp.concatenate([cos, cos, jnp.ones((t, pad), F32)], axis=1)
    sin_t = jnp.concatenate([-sin, sin, jnp.zeros((t, pad), F32)], axis=1)
    return cos_t, sin_t


def _split_w_in(w, d_model):
    widths = (FOX_W, FOX_W, FOX_W, FOX_HEADS, NSA_W) + (NSA_KV_W,) * 6 + (3 * NSA_HEADS, MEM_W, N_BRANCHES * d_model)
    offs = np.concatenate([[0], np.cumsum(widths)])
    seg = lambda i: w[:, offs[i]:offs[i + 1]]
    main = jnp.concatenate([seg(0), seg(1), seg(2), seg(4), seg(13)] + [seg(i) for i in range(5, 11)] + [seg(12)],
                           axis=1).astype(BF16)
    small = jnp.concatenate([seg(3), seg(11), jnp.zeros((w.shape[0], SMALL_W - FOX_HEADS - 3 * NSA_HEADS), w.dtype)],
                            axis=1).astype(BF16)
    return main, small


def _pick(n, candidates):
    for c in candidates:
        if n % c == 0:
            return c
    raise ValueError(f"no tile for {n}")


def kernel(x, mem, w_in, b_f, w_cmp1_k, w_cmp2_k, pe_cmp_k, w_cmp1_v, w_cmp2_v, pe_cmp_v, w_mem_kv, g_mem, w_up_fox, w_up_nsa, w_up_mem, w_o, g_pre_mix, g_post_mix, g_pre_mlp, g_post_mlp, w_mlp1, w_mlp2):
    b, t, d = x.shape
    depth = w_in.shape[0]
    n = b * t
    mlen = mem.shape[1]
    off_kv = OFF_BG + N_BRANCHES * d
    off_mq = off_kv + 6 * NSA_KV_W
    n_main = off_mq + MEM_W

    cos_t, sin_t = _rope_tables(t)
    tm_proj = _pick(n, (1024, 512, 256))
    tn_proj = _pick(n_main, (1280, 1024, 512, 256, 128))
    attn_tile = _pick(t, (512, 256, 128))
    nsa_tile = _pick(t, (256, 128))
    tm_mem = _pick(b * mlen, (1024, 512, 256))

    h = x.reshape(n, d)
    mem2 = mem.reshape(b * mlen, d)
    zero_small = jnp.zeros((d, SMALL_W), BF16)
    for l in range(depth):
        w_main, w_small = _split_w_in(w_in[l], d)
        proj, small = _norm_matmul(h, g_pre_mix[l], w_main, w_small, tm_proj, tn_proj, "in_proj")
        proj = proj.reshape(b, t, n_main)

        ff = small[:, SMALL_FF:SMALL_FF + FOX_HEADS].reshape(b, t, FOX_HEADS).transpose(0, 2, 1)
        c = _fox_cumsum(ff, b_f[l])
        o_fox = _fox_attention(proj, c, attn_tile, attn_tile, 1)

        kc, vc, ske, wkr, svt, wvt = _nsa_prep(
            proj, cos_t, sin_t,
            w_cmp1_k[l].astype(BF16), w_cmp2_k[l].astype(BF16), pe_cmp_k[l],
            w_cmp1_v[l].astype(BF16), w_cmp2_v[l].astype(BF16), pe_cmp_v[l], off_kv, 2 * nsa_tile, nsa_tile)
        qt, o_cmp, sbt = _nsa_cmp(proj, cos_t, sin_t, kc, vc, nsa_tile)
        o_slc = _nsa_slc(qt, sbt, ske, svt, 2 * nsa_tile)
        o_win = _nsa_win(qt, wkr, wvt, nsa_tile)

        mkv, _ = _norm_matmul(mem2, g_mem[l], w_mem_kv[l].astype(BF16), zero_small, tm_mem, 1024, "mem_kv")
        o_mem = _mem_attention(proj, mkv.reshape(b, mlen, 2 * MEM_W), off_mq, attn_tile)

        merged = _merge(o_fox.reshape(n, FOX_W), o_cmp.reshape(n, NSA_W), o_slc.reshape(n, NSA_W),
                        o_win.reshape(n, NSA_W), small, o_mem.reshape(n, MEM_W), proj.reshape(n, n_main),
                        w_up_fox[l].astype(BF16), w_up_nsa[l].astype(BF16), w_up_mem[l].astype(BF16), 512, 1024)
        h = _out_proj(merged, w_o[l].astype(BF16), g_post_mix[l], h, 512)
        h = _mlp(h, g_pre_mlp[l], w_mlp1[l].astype(BF16), w_mlp2[l].astype(BF16), g_post_mlp[l], 512, 1024)
    return h.reshape(b, t, d)
```

```python
import functools

import numpy as np
import jax
import jax.numpy as jnp
from jax import lax
from jax.experimental import pallas as pl
from jax.experimental.pallas import tpu as pltpu

F32 = jnp.float32
BF16 = jnp.bfloat16

HEAD_DIM = 128
FOX_HEADS = 8
NSA_HEADS = 8
NSA_KV_HEADS = 2
NSA_GROUP = NSA_HEADS // NSA_KV_HEADS
MEM_HEADS = 4
MEM_HEAD_DIM = 256
N_BRANCHES = 3
CMP_LEN = 32
CMP_STRIDE = 16
SLC_BLOCK = 64
SLC_TOPK = 16
WINDOW = 512
ROPE_THETA = 500000.0
ROPE_DIM = HEAD_DIM // 4
RMS_EPS = 1e-6
NEG = -1e30
FORCED_SCORE = 1e6
LOG2E = 1.4426950408889634

LANES = 128
VMEM_LIMIT = 56 * 1024 * 1024

FOX_W = FOX_HEADS * HEAD_DIM
NSA_W = NSA_HEADS * HEAD_DIM
NSA_KV_W = NSA_KV_HEADS * HEAD_DIM
MEM_W = MEM_HEADS * MEM_HEAD_DIM
OFF_FQ = 0
OFF_FK = OFF_FQ + FOX_W
OFF_FV = OFF_FK + FOX_W
OFF_NQ = OFF_FV + FOX_W
OFF_BG = OFF_NQ + NSA_W
SMALL_W = LANES
SMALL_FF = 0
SMALL_NG = FOX_HEADS


def _params(sem):
    return pltpu.CompilerParams(dimension_semantics=sem, vmem_limit_bytes=VMEM_LIMIT)


def _rms(x, g):
    return (x * lax.rsqrt(jnp.mean(x * x, axis=-1, keepdims=True) + RMS_EPS)) * g


def _sigmoid(x):
    return 0.5 * jnp.tanh(0.5 * x) + 0.5


def _dot(a, b):
    return jnp.dot(a, b, preferred_element_type=F32)


def _dot_nt(a, b):
    return lax.dot_general(a, b, (((1,), (1,)), ((), ())), preferred_element_type=F32)


def _norm_matmul_kernel(x_ref, g_ref, w_ref, ws_ref, o_ref, os_ref, u_ref):
    @pl.when(pl.program_id(1) == 0)
    def _():
        u_ref[...] = _rms(x_ref[...], g_ref[...]).astype(BF16)
        os_ref[...] = _dot(u_ref[...], ws_ref[...])

    o_ref[...] = _dot(u_ref[...], w_ref[...]).astype(o_ref.dtype)


def _norm_matmul(x, g, w, ws, tm, tn, name):
    m, d = x.shape
    n = w.shape[1]
    return pl.pallas_call(
        _norm_matmul_kernel,
        out_shape=(jax.ShapeDtypeStruct((m, n), BF16), jax.ShapeDtypeStruct((m, SMALL_W), F32)),
        grid=(m // tm, n // tn),
        in_specs=[
            pl.BlockSpec((tm, d), lambda i, j: (i, 0)),
            pl.BlockSpec((1, d), lambda i, j: (0, 0)),
            pl.BlockSpec((d, tn), lambda i, j: (0, j)),
            pl.BlockSpec((d, SMALL_W), lambda i, j: (0, 0)),
        ],
        out_specs=(
            pl.BlockSpec((tm, tn), lambda i, j: (i, j)),
            pl.BlockSpec((tm, SMALL_W), lambda i, j: (i, 0)),
        ),
        scratch_shapes=[pltpu.VMEM((tm, d), BF16)],
        compiler_params=_params(("parallel", "arbitrary")),
        name=name,
    )(x, g.reshape(1, d), w, ws)


def _fox_cumsum_kernel(bf_ref, ff_ref, c_ref):
    x = ff_ref[...] + bf_ref[pl.program_id(1)]
    lf = jnp.minimum(x, 0.0) - jnp.log1p(jnp.exp(-jnp.abs(x)))
    rows, lanes = lf.shape
    lane = lax.broadcasted_iota(jnp.int32, lf.shape, 1)
    row = lax.broadcasted_iota(jnp.int32, lf.shape, 0)
    y = lf
    s = 1
    while s < lanes:
        y = y + jnp.where(lane >= s, pltpu.roll(y, s, axis=1), 0.0)
        s *= 2
    tot = jnp.broadcast_to(y[:, lanes - 1:lanes], lf.shape)
    z = tot
    s = 1
    while s < rows:
        z = z + jnp.where(row >= s, pltpu.roll(z, s, axis=0), 0.0)
        s *= 2
    c_ref[...] = (y + (z - tot)) * LOG2E


def _fox_cumsum(ff, b_f):
    b, h, t = ff.shape
    rows = t // LANES
    out = pl.pallas_call(
        _fox_cumsum_kernel,
        out_shape=jax.ShapeDtypeStruct((b, h, rows, LANES), F32),
        grid=(b, h),
        in_specs=[
            pl.BlockSpec(memory_space=pltpu.SMEM),
            pl.BlockSpec((None, None, rows, LANES), lambda i, j: (i, j, 0, 0)),
        ],
        out_specs=pl.BlockSpec((None, None, rows, LANES), lambda i, j: (i, j, 0, 0)),
        compiler_params=_params(("parallel", "parallel")),
        name="fox_cumsum",
    )(b_f, ff.reshape(b, h, rows, LANES))
    return out.reshape(b, h, t)


def _flash_t_step(carry, s, v_t):
    m, l, acc = carry
    m_new = jnp.maximum(m, jnp.max(s, axis=0, keepdims=True))
    alpha = jnp.exp2(m - m_new)
    p = jnp.exp2(s - m_new)
    l = alpha * l + jnp.sum(p, axis=0, keepdims=True)
    acc = alpha * acc + _dot(v_t, p.astype(BF16))
    return m_new, l, acc


def _flash_t_init(cols, d):
    return (jnp.full((1, cols), -jnp.inf, F32), jnp.zeros((1, cols), F32), jnp.zeros((d, cols), F32))


FOX_C_TERMS = 3


def _fox_attn_kernel(q_ref, k_ref, v_ref, c_ref, o_ref, ke_scr, vt_scr, *, tq, tk, heads, qscale):
    i = pl.program_id(2)
    nk = vt_scr.shape[1]

    @pl.when(i == 0)
    def _():
        row = lax.broadcasted_iota(jnp.int32, (HEAD_DIM, tk), 0)
        for hh in range(heads):
            hs = slice(hh * HEAD_DIM, (hh + 1) * HEAD_DIM)
            for kc in range(nk):
                rows = slice(kc * tk, (kc + 1) * tk)
                vt_scr[hh, kc] = v_ref[rows, hs].astype(F32).T.astype(BF16)
                ke_scr[hh, rows, 0:HEAD_DIM] = k_ref[rows, hs]
                c = c_ref[hh, :, rows]
                terms = jnp.zeros((HEAD_DIM, tk), F32)
                for n in range(FOX_C_TERMS):
                    part = c.astype(BF16).astype(F32)
                    terms = jnp.where(row == n, part, terms)
                    c = c - part
                ke_scr[hh, rows, HEAD_DIM:2 * HEAD_DIM] = terms.T.astype(BF16)

    minus_ones = jnp.where(lax.broadcasted_iota(jnp.int32, (HEAD_DIM, tq), 0) < FOX_C_TERMS, -1.0, 0.0).astype(BF16)
    q_ts = []
    for hh in range(heads):
        q_t = (q_ref[:, hh * HEAD_DIM:(hh + 1) * HEAD_DIM].astype(F32) * qscale).T.astype(BF16)
        q_ts.append(jnp.concatenate([q_t, minus_ones], axis=0))
    n_full = i * (tq // tk)

    def body(kc, states):
        start = pl.multiple_of(kc * tk, tk)
        return tuple(_flash_t_step(states[hh], _dot(ke_scr[hh, pl.ds(start, tk), :], q_ts[hh]), vt_scr[hh, kc])
                     for hh in range(heads))

    states = lax.fori_loop(0, n_full, body, tuple(_flash_t_init(tq, HEAD_DIM) for _ in range(heads)))

    for d in range(tq // tk):
        lo = d * tk
        key = lax.broadcasted_iota(jnp.int32, (tk, tq - lo), 0)
        col = lax.broadcasted_iota(jnp.int32, (tk, tq - lo), 1)
        start = pl.multiple_of((n_full + d) * tk, tk)
        new_states = []
        for hh in range(heads):
            m, l, acc = states[hh]
            s = jnp.where(key <= col, _dot(ke_scr[hh, pl.ds(start, tk), :], q_ts[hh][:, lo:]), NEG)
            upd = _flash_t_step((m[:, lo:], l[:, lo:], acc[:, lo:]), s, vt_scr[hh, n_full + d])
            if lo:
                upd = tuple(jnp.concatenate([old[:, :lo], new], axis=1) for old, new in zip((m, l, acc), upd))
            new_states.append(upd)
        states = new_states

    for hh in range(heads):
        _, l, acc = states[hh]
        o_ref[:, hh * HEAD_DIM:(hh + 1) * HEAD_DIM] = (acc / l).T.astype(o_ref.dtype)


def _fox_attention(proj, c2, tq, tk, heads):
    b, t, _ = proj.shape
    nk = t // tk
    hw = heads * HEAD_DIM
    assert tq % tk == 0 and FOX_HEADS % heads == 0
    return pl.pallas_call(
        functools.partial(_fox_attn_kernel, tq=tq, tk=tk, heads=heads, qscale=HEAD_DIM ** -0.5 * LOG2E),
        out_shape=jax.ShapeDtypeStruct((b, t, FOX_W), BF16),
        grid=(b, FOX_HEADS // heads, t // tq),
        in_specs=[
            pl.BlockSpec((None, tq, hw), lambda bi, h, i: (bi, i, OFF_FQ // hw + h)),
            pl.BlockSpec((None, t, hw), lambda bi, h, i: (bi, 0, OFF_FK // hw + h)),
            pl.BlockSpec((None, t, hw), lambda bi, h, i: (bi, 0, OFF_FV // hw + h)),
            pl.BlockSpec((None, heads, 1, t), lambda bi, h, i: (bi, h, 0, 0)),
        ],
        out_specs=pl.BlockSpec((None, tq, hw), lambda bi, h, i: (bi, i, h)),
        scratch_shapes=[pltpu.VMEM((heads, t, 2 * HEAD_DIM), BF16), pltpu.VMEM((heads, nk, HEAD_DIM, tk), BF16)],
        compiler_params=_params(("parallel", "parallel", "arbitrary")),
        name="fox_attention",
    )(proj, proj, proj, c2.reshape(b, FOX_HEADS, 1, t))


def _rope(x, cos, sin):
    half = ROPE_DIM // 2
    lane = lax.broadcasted_iota(jnp.int32, x.shape, 1)
    swapped = jnp.where(lane < half, pltpu.roll(x, LANES - half, axis=1), pltpu.roll(x, half, axis=1))
    return x * cos + swapped * sin


def _compress(xs_ref, x, w1_ref, w2_ref, pe_ref):
    t = x.shape[0]
    nch = t // CMP_STRIDE
    xs_ref[...] = x
    pe = pe_ref[...]
    first, second = [], []
    for p in range(CMP_STRIDE):
        xp = xs_ref[pl.ds(p, nch, stride=CMP_STRIDE), :]
        first.append((xp + pe[p:p + 1]).astype(BF16))
        second.append((xp + pe[CMP_STRIDE + p:CMP_STRIDE + p + 1]).astype(BF16))
    kw = CMP_STRIDE * HEAD_DIM
    a = _dot(jnp.concatenate(first, axis=1), w1_ref[0:kw, :])
    b = _dot(jnp.concatenate(second, axis=1), w1_ref[kw:2 * kw, :])
    hidden = jax.nn.gelu(a + pltpu.roll(b, nch - 1, axis=0))
    return _dot(hidden.astype(BF16), w2_ref[...])


def _store_transposed_chunks(dst_ref, src_ref):
    nk, _, tile = dst_ref.shape
    for c in range(nk):
        dst_ref[c] = src_ref[c * tile:(c + 1) * tile, :].astype(F32).T.astype(dst_ref.dtype)


def _nsa_prep_kernel(ck_ref, cv_ref, sk_ref, sv_ref, wk_ref, wv_ref, cos_ref, sin_ref,
                     w1k_ref, w2k_ref, pek_ref, w1v_ref, w2v_ref, pev_ref,
                     kc_ref, vc_ref, ske_ref, wkr_ref, svt_ref, wvt_ref, xs_ref):
    cos = cos_ref[...]
    sin = sin_ref[...]
    t = cos.shape[0]
    _store_transposed_chunks(svt_ref, sv_ref)
    _store_transposed_chunks(wvt_ref, wv_ref)
    kc_ref[...] = _compress(xs_ref, _rope(ck_ref[...].astype(F32), cos, sin), w1k_ref, w2k_ref, pek_ref).astype(BF16)
    vc_ref[...] = _compress(xs_ref, cv_ref[...].astype(F32), w1v_ref, w2v_ref, pev_ref).astype(BF16)
    ske_ref[:, 0:HEAD_DIM] = _rope(sk_ref[...].astype(F32), cos, sin).astype(BF16)
    row = lax.broadcasted_iota(jnp.int32, (t, LANES), 0)
    lane = lax.broadcasted_iota(jnp.int32, (t, LANES), 1)
    ske_ref[:, HEAD_DIM:2 * HEAD_DIM] = jnp.where(row // SLC_BLOCK == lane, 1.0, 0.0).astype(BF16)
    wkr_ref[...] = _rope(wk_ref[...].astype(F32), cos, sin).astype(BF16)


def _nsa_prep(proj, cos, sin, w1k, w2k, pek, w1v, w2v, pev, off_kv, slc_tk, win_tk):
    b, t, _ = proj.shape
    g = NSA_KV_HEADS
    nch = t // CMP_STRIDE
    blk = lambda sec: (lambda bi, gi: (bi, 0, (off_kv + sec * NSA_KV_W) // HEAD_DIM + gi))
    full = lambda shape: pl.BlockSpec(shape, lambda bi, gi: (0,) * len(shape))
    per_head = lambda shape: pl.BlockSpec((None, None) + shape, lambda bi, gi: (bi, gi) + (0,) * len(shape))
    return pl.pallas_call(
        _nsa_prep_kernel,
        out_shape=(
            jax.ShapeDtypeStruct((b, g, nch, HEAD_DIM), BF16),
            jax.ShapeDtypeStruct((b, g, nch, HEAD_DIM), BF16),
            jax.ShapeDtypeStruct((b, g, t, 2 * HEAD_DIM), BF16),
            jax.ShapeDtypeStruct((b, g, t, HEAD_DIM), BF16),
            jax.ShapeDtypeStruct((b, g, t // slc_tk, HEAD_DIM, slc_tk), BF16),
            jax.ShapeDtypeStruct((b, g, t // win_tk, HEAD_DIM, win_tk), BF16),
        ),
        grid=(b, g),
        in_specs=[pl.BlockSpec((None, t, HEAD_DIM), blk(sec)) for sec in range(6)] + [
            full((t, LANES)), full((t, LANES)),
            full(w1k.shape), full(w2k.shape), full(pek.shape),
            full(w1v.shape), full(w2v.shape), full(pev.shape),
        ],
        out_specs=(
            per_head((nch, HEAD_DIM)), per_head((nch, HEAD_DIM)),
            per_head((t, 2 * HEAD_DIM)), per_head((t, HEAD_DIM)),
            per_head((t // slc_tk, HEAD_DIM, slc_tk)), per_head((t // win_tk, HEAD_DIM, win_tk)),
        ),
        scratch_shapes=[pltpu.VMEM((t, HEAD_DIM), F32)],
        compiler_params=_params(("parallel", "parallel")),
        name="nsa_prep",
    )(proj, proj, proj, proj, proj, proj, cos, sin, w1k, w2k, pek, w1v, w2v, pev)


def _nsa_cmp_kernel(q_ref, cos_ref, sin_ref, kc_ref, vc_ref, gate_ref, qt_ref, ocmp_ref, sbt_ref, *, tq, scale):
    q0 = pl.program_id(2) * tq
    gates = _sigmoid(gate_ref[...])
    cos = cos_ref[...]
    sin = sin_ref[...]
    kc = kc_ref[...]
    vc = vc_ref[...]
    nch = kc.shape[0]
    n_real = nch - 1
    nsel = (nch * CMP_STRIDE) // SLC_BLOCK
    tpos = q0 + lax.broadcasted_iota(jnp.int32, (tq, nch), 0)
    nidx = lax.broadcasted_iota(jnp.int32, (tq, nch), 1)
    valid = (nidx * CMP_STRIDE + (CMP_LEN - 1) <= tpos) & (nidx < n_real)

    psum = jnp.zeros((tq, nch), F32)
    for j in range(NSA_GROUP):
        sl = slice(j * HEAD_DIM, (j + 1) * HEAD_DIM)
        qf = _rope(q_ref[:, sl].astype(F32), cos, sin)
        qt_ref[j] = (qf * (scale * LOG2E)).T.astype(BF16)
        qj = qf.astype(BF16)
        lg = jnp.where(valid, _dot_nt(qj, kc) * scale, NEG)
        e = jnp.where(valid, jnp.exp(lg - jnp.max(lg, axis=-1, keepdims=True)), 0.0)
        den = jnp.sum(e, axis=-1, keepdims=True)
        inv = 1.0 / jnp.where(den > 0.0, den, 1.0)
        ocmp_ref[:, sl] = _dot((e * (inv * gates[:, j:j + 1])).astype(BF16), vc).astype(ocmp_ref.dtype)
        psum = psum + e * inv

    ratio = SLC_BLOCK // CMP_STRIDE
    jrow = lax.broadcasted_iota(jnp.int32, (nsel, nch), 0)
    ncol = lax.broadcasted_iota(jnp.int32, (nsel, nch), 1)
    lo_n = jrow * ratio - (CMP_LEN - 1) // CMP_STRIDE
    hi_n = jrow * ratio + (SLC_BLOCK - 1) // CMP_STRIDE
    overlap_t = jnp.where((ncol >= lo_n) & (ncol <= hi_n) & (ncol < n_real), 1.0, 0.0).astype(BF16)
    p_hi = psum.astype(BF16)
    r1 = psum - p_hi.astype(F32)
    p_mid = r1.astype(BF16)
    p_lo = (r1 - p_mid.astype(F32)).astype(BF16)
    pslc_t = _dot_nt(overlap_t, p_hi) + _dot_nt(overlap_t, p_mid) + _dot_nt(overlap_t, p_lo)

    blk = lax.broadcasted_iota(jnp.int32, (nsel, tq), 0)
    cur = (q0 + lax.broadcasted_iota(jnp.int32, (nsel, tq), 1)) // SLC_BLOCK
    forced = (blk == 0) | (blk == cur) | (blk == cur - 1)
    score = jnp.where(forced, FORCED_SCORE, jnp.where(blk <= cur, pslc_t, -1.0))

    rank = jnp.zeros((nsel, tq), F32)
    for i in range(nsel):
        si = score[i:i + 1, :]
        beats = (si > score) | ((si == score) & (blk > i))
        rank = rank + jnp.where(beats, 1.0, 0.0)
    k_top = min(SLC_TOPK, nsel)
    bias_t = jnp.where(rank < k_top, 0.0, NEG)
    sbt_ref[...] = jnp.concatenate([bias_t, jnp.zeros((LANES - nsel, tq), F32)], axis=0).astype(BF16)


def _nsa_cmp(proj, cos, sin, kc, vc, gates_cmp, tq):
    b, t, _ = proj.shape
    g = NSA_KV_HEADS
    gw = NSA_GROUP * HEAD_DIM
    nch = kc.shape[2]
    assert (nch * CMP_STRIDE) // SLC_BLOCK <= LANES // 2
    return pl.pallas_call(
        functools.partial(_nsa_cmp_kernel, tq=tq, scale=HEAD_DIM ** -0.5),
        out_shape=(
            jax.ShapeDtypeStruct((b, g, NSA_GROUP, HEAD_DIM, t), BF16),
            jax.ShapeDtypeStruct((b, t, NSA_W), F32),
            jax.ShapeDtypeStruct((b, g, LANES, t), BF16),
        ),
        grid=(b, g, t // tq),
        in_specs=[
            pl.BlockSpec((None, tq, gw), lambda bi, gi, i: (bi, i, OFF_NQ // gw + gi)),
            pl.BlockSpec((tq, LANES), lambda bi, gi, i: (i, 0)),
            pl.BlockSpec((tq, LANES), lambda bi, gi, i: (i, 0)),
            pl.BlockSpec((None, None, nch, HEAD_DIM), lambda bi, gi, i: (bi, gi, 0, 0)),
            pl.BlockSpec((None, None, nch, HEAD_DIM), lambda bi, gi, i: (bi, gi, 0, 0)),
            pl.BlockSpec((None, None, tq, NSA_GROUP), lambda bi, gi, i: (bi, gi, i, 0)),
        ],
        out_specs=(
            pl.BlockSpec((None, None, NSA_GROUP, HEAD_DIM, tq), lambda bi, gi, i: (bi, gi, 0, 0, i)),
            pl.BlockSpec((None, tq, gw), lambda bi, gi, i: (bi, i, gi)),
            pl.BlockSpec((None, None, LANES, tq), lambda bi, gi, i: (bi, gi, 0, i)),
        ),
        compiler_params=_params(("parallel", "parallel", "parallel")),
        name="nsa_cmp_select",
    )(proj, cos, sin, kc, vc, gates_cmp)


def _heads_on_lanes(qt_ref, extra=None):
    parts = []
    for j in range(NSA_GROUP):
        parts.append(qt_ref[j] if extra is None else jnp.concatenate([qt_ref[j], extra], axis=0))
    return jnp.concatenate(parts, axis=1)


def _store_heads(o_ref, o_t, tile):
    for j in range(NSA_GROUP):
        o_ref[:, j * HEAD_DIM:(j + 1) * HEAD_DIM] = o_t[:, j * tile:(j + 1) * tile].T.astype(o_ref.dtype)


def _gate_row(gate_ref):
    g = _sigmoid(gate_ref[...])
    return jnp.concatenate([g[j:j + 1, :] for j in range(NSA_GROUP)], axis=1)


def _nsa_slc_kernel(qt_ref, sbt_ref, ke_ref, vt_ref, gate_ref, o_ref, *, tq, tk):
    i = pl.program_id(2)
    q_t = _heads_on_lanes(qt_ref, sbt_ref[...])
    cols = NSA_GROUP * tq
    n_full = (i * tq) // tk

    def logits(kc):
        k = ke_ref[pl.ds(pl.multiple_of(kc * tk, tk), tk), :]
        return _dot(k, q_t)

    state = lax.fori_loop(0, n_full, lambda kc, c: _flash_t_step(c, logits(kc), vt_ref[kc]),
                          _flash_t_init(cols, HEAD_DIM))
    key = lax.broadcasted_iota(jnp.int32, (tk, cols), 0) + n_full * tk
    qry = lax.broadcasted_iota(jnp.int32, (tk, cols), 1) % tq + i * tq
    for d in range(max(tq // tk, 1)):
        s = jnp.where(key + d * tk <= qry, logits(n_full + d), NEG)
        state = _flash_t_step(state, s, vt_ref[n_full + d])
    _, l, acc = state
    _store_heads(o_ref, acc * (_gate_row(gate_ref) / l), tq)


def _nsa_slc(qt, sbt, ske, svt, gates_t, tq):
    b, g, _, _, t = qt.shape
    gw = NSA_GROUP * HEAD_DIM
    nk, tk = svt.shape[2], svt.shape[4]
    assert tq % tk == 0 or tk % tq == 0
    return pl.pallas_call(
        functools.partial(_nsa_slc_kernel, tq=tq, tk=tk),
        out_shape=jax.ShapeDtypeStruct((b, t, NSA_W), F32),
        grid=(b, g, t // tq),
        in_specs=[
            pl.BlockSpec((None, None, NSA_GROUP, HEAD_DIM, tq), lambda bi, gi, i: (bi, gi, 0, 0, i)),
            pl.BlockSpec((None, None, LANES, tq), lambda bi, gi, i: (bi, gi, 0, i)),
            pl.BlockSpec((None, None, t, 2 * HEAD_DIM), lambda bi, gi, i: (bi, gi, 0, 0)),
            pl.BlockSpec((None, None, nk, HEAD_DIM, tk), lambda bi, gi, i: (bi, gi, 0, 0, 0)),
            pl.BlockSpec((None, None, None, NSA_GROUP, tq), lambda bi, gi, i: (bi, 1, gi, 0, i)),
        ],
        out_specs=pl.BlockSpec((None, tq, gw), lambda bi, gi, i: (bi, i, gi)),
        compiler_params=_params(("parallel", "parallel", "arbitrary")),
        name="nsa_selected",
    )(qt, sbt, ske, svt, gates_t)


def _nsa_win_kernel(qt_ref, k_ref, vt_ref, gate_ref, o_ref, *, tile):
    i = pl.program_id(2)
    q_t = _heads_on_lanes(qt_ref)
    cols = NSA_GROUP * tile
    back = WINDOW // tile
    span = (back + 1) * tile
    chunks = [jnp.maximum(i - back + j, 0) for j in range(back + 1)]
    k = jnp.concatenate([k_ref[pl.ds(pl.multiple_of(c * tile, tile), tile), :] for c in chunks], axis=0)
    v_t = jnp.concatenate([vt_ref[c] for c in chunks], axis=1)
    key = lax.broadcasted_iota(jnp.int32, (span, cols), 0) + (i - back) * tile
    qry = lax.broadcasted_iota(jnp.int32, (span, cols), 1) % tile + i * tile
    rel = qry - key
    s = jnp.where((rel >= 0) & (rel < WINDOW) & (key >= 0), _dot(k, q_t), NEG)
    p = jnp.exp2(s - jnp.max(s, axis=0, keepdims=True))
    l = jnp.sum(p, axis=0, keepdims=True)
    _store_heads(o_ref, _dot(v_t, p.astype(BF16)) * (_gate_row(gate_ref) / l), tile)


def _nsa_win(qt, wkr, wvt, gates_t, tile):
    b, g, _, _, t = qt.shape
    gw = NSA_GROUP * HEAD_DIM
    nk = t // tile
    assert WINDOW % tile == 0
    return pl.pallas_call(
        functools.partial(_nsa_win_kernel, tile=tile),
        out_shape=jax.ShapeDtypeStruct((b, t, NSA_W), F32),
        grid=(b, g, nk),
        in_specs=[
            pl.BlockSpec((None, None, NSA_GROUP, HEAD_DIM, tile), lambda bi, gi, i: (bi, gi, 0, 0, i)),
            pl.BlockSpec((None, None, t, HEAD_DIM), lambda bi, gi, i: (bi, gi, 0, 0)),
            pl.BlockSpec((None, None, nk, HEAD_DIM, tile), lambda bi, gi, i: (bi, gi, 0, 0, 0)),
            pl.BlockSpec((None, None, None, NSA_GROUP, tile), lambda bi, gi, i: (bi, 2, gi, 0, i)),
        ],
        out_specs=pl.BlockSpec((None, tile, gw), lambda bi, gi, i: (bi, i, gi)),
        compiler_params=_params(("parallel", "parallel", "parallel")),
        name="nsa_window",
    )(qt, wkr, wvt, gates_t)


def _mem_attn_kernel(q_ref, k_ref, v_ref, o_ref, *, scale):
    s = _dot_nt(q_ref[...], k_ref[...]) * scale
    e = jnp.exp(s - jnp.max(s, axis=-1, keepdims=True))
    p = e / jnp.sum(e, axis=-1, keepdims=True)
    o_ref[...] = _dot(p.astype(BF16), v_ref[...]).astype(o_ref.dtype)


def _mem_attention(proj, mkv, off_mq, tq):
    b, t, _ = proj.shape
    m = mkv.shape[1]
    d = MEM_HEAD_DIM
    return pl.pallas_call(
        functools.partial(_mem_attn_kernel, scale=d ** -0.5),
        out_shape=jax.ShapeDtypeStruct((b, t, MEM_W), BF16),
        grid=(b, MEM_HEADS, t // tq),
        in_specs=[
            pl.BlockSpec((None, tq, d), lambda bi, h, i: (bi, i, off_mq // d + h)),
            pl.BlockSpec((None, m, d), lambda bi, h, i: (bi, 0, h)),
            pl.BlockSpec((None, m, d), lambda bi, h, i: (bi, 0, MEM_HEADS + h)),
        ],
        out_specs=pl.BlockSpec((None, tq, d), lambda bi, h, i: (bi, i, h)),
        compiler_params=_params(("parallel", "parallel", "parallel")),
        name="mem_attention",
    )(proj, mkv, mkv)


def _merge_kernel(ofox_ref, ocmp_ref, oslc_ref, owin_ref, omem_ref,
                  bg0_ref, bg1_ref, bg2_ref, wf_ref, wn_ref, wm_ref, o_ref, onsa_ref):
    @pl.when(pl.program_id(1) == 0)
    def _():
        onsa_ref[...] = (ocmp_ref[...] + oslc_ref[...] + owin_ref[...]).astype(BF16)

    sig = lambda r: _sigmoid(r[...].astype(F32))
    o_ref[...] = (sig(bg0_ref) * _dot(ofox_ref[...], wf_ref[...])
                  + sig(bg1_ref) * _dot(onsa_ref[...], wn_ref[...])
                  + sig(bg2_ref) * _dot(omem_ref[...], wm_ref[...])).astype(o_ref.dtype)


def _merge(ofox, ocmp, oslc, owin, omem, proj, wf, wn, wm, tm, tn):
    m, dm = ofox.shape[0], wf.shape[1]
    row = lambda w: pl.BlockSpec((tm, w), lambda i, j: (i, 0))
    gate = lambda br: pl.BlockSpec((tm, tn), lambda i, j: (i, (OFF_BG + br * dm) // tn + j))
    wspec = lambda w: pl.BlockSpec((w.shape[0], tn), lambda i, j: (0, j))
    return pl.pallas_call(
        _merge_kernel,
        out_shape=jax.ShapeDtypeStruct((m, dm), BF16),
        grid=(m // tm, dm // tn),
        in_specs=[row(FOX_W), row(NSA_W), row(NSA_W), row(NSA_W), row(MEM_W),
                  gate(0), gate(1), gate(2), wspec(wf), wspec(wn), wspec(wm)],
        out_specs=pl.BlockSpec((tm, tn), lambda i, j: (i, j)),
        scratch_shapes=[pltpu.VMEM((tm, NSA_W), BF16)],
        compiler_params=_params(("parallel", "arbitrary")),
        name="branch_merge",
    )(ofox, ocmp, oslc, owin, omem, proj, proj, proj, wf, wn, wm)


def _out_proj_kernel(m_ref, w_ref, g_ref, h_ref, o_ref):
    o_ref[...] = h_ref[...] + _rms(_dot(m_ref[...], w_ref[...]), g_ref[...])


def _out_proj(merged, w, g, h, tm):
    m, d = h.shape
    return pl.pallas_call(
        _out_proj_kernel,
        out_shape=jax.ShapeDtypeStruct((m, d), F32),
        grid=(m // tm,),
        in_specs=[
            pl.BlockSpec((tm, d), lambda i: (i, 0)),
            pl.BlockSpec((d, d), lambda i: (0, 0)),
            pl.BlockSpec((1, d), lambda i: (0, 0)),
            pl.BlockSpec((tm, d), lambda i: (i, 0)),
        ],
        out_specs=pl.BlockSpec((tm, d), lambda i: (i, 0)),
        compiler_params=_params(("parallel",)),
        name="out_proj",
    )(merged, w, g.reshape(1, d), h)


def _mlp_kernel(h_ref, gpre_ref, w1_ref, w2_ref, gpost_ref, o_ref, u_ref, acc_ref):
    f = pl.program_id(1)

    @pl.when(f == 0)
    def _():
        u_ref[...] = _rms(h_ref[...], gpre_ref[...]).astype(BF16)
        acc_ref[...] = jnp.zeros_like(acc_ref)

    a = jnp.square(jnp.maximum(_dot(u_ref[...], w1_ref[...]), 0.0))
    acc_ref[...] += _dot(a.astype(BF16), w2_ref[...])

    @pl.when(f == pl.num_programs(1) - 1)
    def _():
        o_ref[...] = h_ref[...] + _rms(acc_ref[...], gpost_ref[...])


def _mlp(h, gpre, w1, w2, gpost, tm, tf):
    m, d = h.shape
    dff = w1.shape[1]
    return pl.pallas_call(
        _mlp_kernel,
        out_shape=jax.ShapeDtypeStruct((m, d), F32),
        grid=(m // tm, dff // tf),
        in_specs=[
            pl.BlockSpec((tm, d), lambda i, f: (i, 0)),
            pl.BlockSpec((1, d), lambda i, f: (0, 0)),
            pl.BlockSpec((d, tf), lambda i, f: (0, f)),
            pl.BlockSpec((tf, d), lambda i, f: (f, 0)),
            pl.BlockSpec((1, d), lambda i, f: (0, 0)),
        ],
        out_specs=pl.BlockSpec((tm, d), lambda i, f: (i, 0)),
        scratch_shapes=[pltpu.VMEM((tm, d), BF16), pltpu.VMEM((tm, d), F32)],
        compiler_params=_params(("parallel", "arbitrary")),
        name="mlp",
    )(h, gpre.reshape(1, d), w1, w2, gpost.reshape(1, d))


def _rope_tables(t):
    half = ROPE_DIM // 2
    inv_freq = ROPE_THETA ** (-jnp.arange(half, dtype=F32) / half)
    ang = jnp.arange(t, dtype=jnp.int32).astype(F32)[:, None] * inv_freq[None, :]
    cos, sin = jnp.cos(ang), jnp.sin(ang)
    pad = HEAD_DIM - ROPE_DIM
    cos_t = jnp.concatenate([cos, cos, jnp.ones((t, pad), F32)], axis=1)
    sin_t = jnp.concatenate([-sin, sin, jnp.zeros((t, pad), F32)], axis=1)
    return cos_t, sin_t


def _split_w_in(w, d_model):
    widths = (FOX_W, FOX_W, FOX_W, FOX_HEADS, NSA_W) + (NSA_KV_W,) * 6 + (3 * NSA_HEADS, MEM_W, N_BRANCHES * d_model)
    offs = np.concatenate([[0], np.cumsum(widths)])
    seg = lambda i: w[:, offs[i]:offs[i + 1]]
    main = jnp.concatenate([seg(0), seg(1), seg(2), seg(4), seg(13)] + [seg(i) for i in range(5, 11)] + [seg(12)],
                           axis=1).astype(BF16)
    small = jnp.concatenate([seg(3), seg(11), jnp.zeros((w.shape[0], SMALL_W - FOX_HEADS - 3 * NSA_HEADS), w.dtype)],
                            axis=1).astype(BF16)
    return main, small


def _pick(n, candidates):
    for c in candidates:
        if n % c == 0:
            return c
    raise ValueError(f"no tile for {n}")


def kernel(x, mem, w_in, b_f, w_cmp1_k, w_cmp2_k, pe_cmp_k, w_cmp1_v, w_cmp2_v, pe_cmp_v, w_mem_kv, g_mem, w_up_fox, w_up_nsa, w_up_mem, w_o, g_pre_mix, g_post_mix, g_pre_mlp, g_post_mlp, w_mlp1, w_mlp2):
    b, t, d = x.shape
    depth = w_in.shape[0]
    n = b * t
    mlen = mem.shape[1]
    off_kv = OFF_BG + N_BRANCHES * d
    off_mq = off_kv + 6 * NSA_KV_W
    n_main = off_mq + MEM_W

    cos_t, sin_t = _rope_tables(t)
    tm_proj = _pick(n, (1024, 512, 256))
    tn_proj = _pick(n_main, (1280, 1024, 512, 256, 128))
    attn_tile = _pick(t, (512, 256, 128))
    nsa_tile = _pick(t, (256, 128))
    tm_mem = _pick(b * mlen, (1024, 512, 256))

    h = x.reshape(n, d)
    mem2 = mem.reshape(b * mlen, d)
    zero_small = jnp.zeros((d, SMALL_W), BF16)
    for l in range(depth):
        w_main, w_small = _split_w_in(w_in[l], d)
        proj, small = _norm_matmul(h, g_pre_mix[l], w_main, w_small, tm_proj, tn_proj, "in_proj")
        proj = proj.reshape(b, t, n_main)

        ff = small[:, SMALL_FF:SMALL_FF + FOX_HEADS].reshape(b, t, FOX_HEADS).transpose(0, 2, 1)
        c = _fox_cumsum(ff, b_f[l])
        o_fox = _fox_attention(proj, c, 2048, 512, 1)

        kc, vc, ske, wkr, svt, wvt = _nsa_prep(
            proj, cos_t, sin_t,
            w_cmp1_k[l].astype(BF16), w_cmp2_k[l].astype(BF16), pe_cmp_k[l],
            w_cmp1_v[l].astype(BF16), w_cmp2_v[l].astype(BF16), pe_cmp_v[l], off_kv, 2 * nsa_tile, nsa_tile)
        ng = small[:, SMALL_NG:SMALL_NG + 3 * NSA_HEADS].reshape(b, t, 3, NSA_KV_HEADS, NSA_GROUP)
        gates_cmp = ng[:, :, 0].transpose(0, 2, 1, 3)
        gates_t = ng.transpose(0, 2, 3, 4, 1)
        qt, o_cmp, sbt = _nsa_cmp(proj, cos_t, sin_t, kc, vc, gates_cmp, nsa_tile)
        o_slc = _nsa_slc(qt, sbt, ske, svt, gates_t, 2 * nsa_tile)
        o_win = _nsa_win(qt, wkr, wvt, gates_t, nsa_tile)

        mkv, _ = _norm_matmul(mem2, g_mem[l], w_mem_kv[l].astype(BF16), zero_small, tm_mem, 1024, "mem_kv")
        o_mem = _mem_attention(proj, mkv.reshape(b, mlen, 2 * MEM_W), off_mq, attn_tile)

        merged = _merge(o_fox.reshape(n, FOX_W), o_cmp.reshape(n, NSA_W), o_slc.reshape(n, NSA_W),
                        o_win.reshape(n, NSA_W), o_mem.reshape(n, MEM_W), proj.reshape(n, n_main),
                        w_up_fox[l].astype(BF16), w_up_nsa[l].astype(BF16), w_up_mem[l].astype(BF16), 512, 1024)
        h = _out_proj(merged, w_o[l].astype(BF16), g_post_mix[l], h, 512)
        h = _mlp(h, g_pre_mlp[l], w_mlp1[l].astype(BF16), w_mlp2[l].astype(BF16), g_post_mlp[l], 512, 1024)
    return h.reshape(b, t, d)
```

```python
import functools

import numpy as np
import jax
import jax.numpy as jnp
from jax import lax
from jax.experimental import pallas as pl
from jax.experimental.pallas import tpu as pltpu

F32 = jnp.float32
BF16 = jnp.bfloat16

HEAD_DIM = 128
FOX_HEADS = 8
NSA_HEADS = 8
NSA_KV_HEADS = 2
NSA_GROUP = NSA_HEADS // NSA_KV_HEADS
MEM_HEADS = 4
MEM_HEAD_DIM = 256
N_BRANCHES = 3
CMP_LEN = 32
CMP_STRIDE = 16
SLC_BLOCK = 64
SLC_TOPK = 16
WINDOW = 512
ROPE_THETA = 500000.0
ROPE_DIM = HEAD_DIM // 4
RMS_EPS = 1e-6
NEG = -1e30
FORCED_SCORE = 1e6
LOG2E = 1.4426950408889634

LANES = 128
SUBLANES = 8
VMEM_LIMIT = 56 * 1024 * 1024

FOX_W = FOX_HEADS * HEAD_DIM
NSA_W = NSA_HEADS * HEAD_DIM
NSA_KV_W = NSA_KV_HEADS * HEAD_DIM
MEM_W = MEM_HEADS * MEM_HEAD_DIM
OFF_FQ = 0
OFF_FK = OFF_FQ + FOX_W
OFF_FV = OFF_FK + FOX_W
OFF_NQ = OFF_FV + FOX_W
OFF_BG = OFF_NQ + NSA_W
SMALL_W = LANES
SMALL_FF = 0
SMALL_NG = FOX_HEADS


def _params(sem):
    return pltpu.CompilerParams(dimension_semantics=sem, vmem_limit_bytes=VMEM_LIMIT)


def _rms(x, g):
    return (x * lax.rsqrt(jnp.mean(x * x, axis=-1, keepdims=True) + RMS_EPS)) * g


def _sigmoid(x):
    return 0.5 * jnp.tanh(0.5 * x) + 0.5


def _dot(a, b):
    return jnp.dot(a, b, preferred_element_type=F32)


def _dot_nt(a, b):
    return lax.dot_general(a, b, (((1,), (1,)), ((), ())), preferred_element_type=F32)


def _cast_kernel(x_ref, o_ref):
    o_ref[...] = x_ref[...].astype(o_ref.dtype)


def _layer_bf16(w, layer, tr):
    _, r, c = w.shape
    return pl.pallas_call(
        _cast_kernel,
        out_shape=jax.ShapeDtypeStruct((r, c), BF16),
        grid=(r // tr,),
        in_specs=[pl.BlockSpec((None, tr, c), lambda i: (layer, i, 0))],
        out_specs=pl.BlockSpec((tr, c), lambda i: (i, 0)),
        compiler_params=_params(("parallel",)),
        name="weight_bf16",
    )(w)


def _norm_matmul_kernel(x_ref, g_ref, w_ref, *rest, narrow):
    if narrow:
        ws_ref, o_ref, os_ref, u_ref = rest
    else:
        o_ref, u_ref = rest

    @pl.when(pl.program_id(1) == 0)
    def _():
        u_ref[...] = _rms(x_ref[...], g_ref[...]).astype(BF16)
        if narrow:
            os_ref[...] = _dot(u_ref[...], ws_ref[...])

    o_ref[...] = _dot(u_ref[...], w_ref[...]).astype(o_ref.dtype)


def _norm_matmul(x, g, w, ws, tm, tn, name):
    m, d = x.shape
    n = w.shape[1]
    narrow = ws is not None
    in_specs = [
        pl.BlockSpec((tm, d), lambda i, j: (i, 0)),
        pl.BlockSpec((1, d), lambda i, j: (0, 0)),
        pl.BlockSpec((d, tn), lambda i, j: (0, j)),
    ]
    out_shape = [jax.ShapeDtypeStruct((m, n), BF16)]
    out_specs = [pl.BlockSpec((tm, tn), lambda i, j: (i, j))]
    args = [x, g.reshape(1, d), w]
    if narrow:
        in_specs.append(pl.BlockSpec((d, SMALL_W), lambda i, j: (0, 0)))
        out_shape.append(jax.ShapeDtypeStruct((m, SMALL_W), F32))
        out_specs.append(pl.BlockSpec((tm, SMALL_W), lambda i, j: (i, 0)))
        args.append(ws)
    return pl.pallas_call(
        functools.partial(_norm_matmul_kernel, narrow=narrow),
        out_shape=tuple(out_shape),
        grid=(m // tm, n // tn),
        in_specs=in_specs,
        out_specs=tuple(out_specs),
        scratch_shapes=[pltpu.VMEM((tm, d), BF16)],
        compiler_params=_params(("parallel", "arbitrary")),
        name=name,
    )(*args)


def _fox_cumsum_kernel(bf_ref, ff_ref, c_ref):
    x = ff_ref[...] + bf_ref[pl.program_id(1)]
    lf = jnp.minimum(x, 0.0) - jnp.log1p(jnp.exp(-jnp.abs(x)))
    rows, lanes = lf.shape
    lane = lax.broadcasted_iota(jnp.int32, lf.shape, 1)
    row = lax.broadcasted_iota(jnp.int32, lf.shape, 0)
    y = lf
    s = 1
    while s < lanes:
        y = y + jnp.where(lane >= s, pltpu.roll(y, s, axis=1), 0.0)
        s *= 2
    tot = jnp.broadcast_to(y[:, lanes - 1:lanes], lf.shape)
    z = tot
    s = 1
    while s < rows:
        z = z + jnp.where(row >= s, pltpu.roll(z, s, axis=0), 0.0)
        s *= 2
    c_ref[...] = (y + (z - tot)) * LOG2E


def _fox_cumsum(ff, b_f):
    b, h, t = ff.shape
    rows = t // LANES
    out = pl.pallas_call(
        _fox_cumsum_kernel,
        out_shape=jax.ShapeDtypeStruct((b, h, rows, LANES), F32),
        grid=(b, h),
        in_specs=[
            pl.BlockSpec(memory_space=pltpu.SMEM),
            pl.BlockSpec((None, None, rows, LANES), lambda i, j: (i, j, 0, 0)),
        ],
        out_specs=pl.BlockSpec((None, None, rows, LANES), lambda i, j: (i, j, 0, 0)),
        compiler_params=_params(("parallel", "parallel")),
        name="fox_cumsum",
    )(b_f, ff.reshape(b, h, rows, LANES))
    return out.reshape(b, h, t)


def _flash_t_step(carry, s, v_t):
    m, l, acc = carry
    m_new = jnp.maximum(m, jnp.max(s, axis=0, keepdims=True))
    alpha = jnp.exp2(m - m_new)
    p = jnp.exp2(s - m_new)
    l = alpha * l + jnp.sum(p, axis=0, keepdims=True)
    acc = alpha * acc + _dot(v_t, p.astype(BF16))
    return m_new, l, acc


def _flash_t_init(cols, d):
    return (jnp.full((1, cols), -jnp.inf, F32), jnp.zeros((1, cols), F32), jnp.zeros((d, cols), F32))


FOX_C_TERMS = 3


def _fox_attn_kernel(q_ref, k_ref, v_ref, c_ref, o_ref, ke_scr, vt_scr, *, tq, tk, heads, qscale):
    i = pl.program_id(2)
    nk = vt_scr.shape[1]

    @pl.when(i == 0)
    def _():
        row = lax.broadcasted_iota(jnp.int32, (HEAD_DIM, tk), 0)
        for hh in range(heads):
            hs = slice(hh * HEAD_DIM, (hh + 1) * HEAD_DIM)
            for kc in range(nk):
                rows = slice(kc * tk, (kc + 1) * tk)
                vt_scr[hh, kc] = v_ref[rows, hs].astype(F32).T.astype(BF16)
                ke_scr[hh, rows, 0:HEAD_DIM] = k_ref[rows, hs]
                c = c_ref[hh, :, rows]
                terms = jnp.zeros((HEAD_DIM, tk), F32)
                for n in range(FOX_C_TERMS):
                    part = c.astype(BF16).astype(F32)
                    terms = jnp.where(row == n, part, terms)
                    c = c - part
                ke_scr[hh, rows, HEAD_DIM:2 * HEAD_DIM] = terms.T.astype(BF16)

    minus_ones = jnp.where(lax.broadcasted_iota(jnp.int32, (HEAD_DIM, tq), 0) < FOX_C_TERMS, -1.0, 0.0).astype(BF16)
    q_ts = []
    for hh in range(heads):
        q_t = (q_ref[:, hh * HEAD_DIM:(hh + 1) * HEAD_DIM].astype(F32) * qscale).T.astype(BF16)
        q_ts.append(jnp.concatenate([q_t, minus_ones], axis=0))
    n_full = i * (tq // tk)

    def body(kc, states):
        start = pl.multiple_of(kc * tk, tk)
        return tuple(_flash_t_step(states[hh], _dot(ke_scr[hh, pl.ds(start, tk), :], q_ts[hh]), vt_scr[hh, kc])
                     for hh in range(heads))

    states = lax.fori_loop(0, n_full, body, tuple(_flash_t_init(tq, HEAD_DIM) for _ in range(heads)))

    for d in range(tq // tk):
        lo = d * tk
        key = lax.broadcasted_iota(jnp.int32, (tk, tq - lo), 0)
        col = lax.broadcasted_iota(jnp.int32, (tk, tq - lo), 1)
        start = pl.multiple_of((n_full + d) * tk, tk)
        new_states = []
        for hh in range(heads):
            m, l, acc = states[hh]
            s = jnp.where(key <= col, _dot(ke_scr[hh, pl.ds(start, tk), :], q_ts[hh][:, lo:]), NEG)
            upd = _flash_t_step((m[:, lo:], l[:, lo:], acc[:, lo:]), s, vt_scr[hh, n_full + d])
            if lo:
                upd = tuple(jnp.concatenate([old[:, :lo], new], axis=1) for old, new in zip((m, l, acc), upd))
            new_states.append(upd)
        states = new_states

    for hh in range(heads):
        _, l, acc = states[hh]
        o_ref[:, hh * HEAD_DIM:(hh + 1) * HEAD_DIM] = (acc / l).T.astype(o_ref.dtype)


def _fox_attention(proj, c2, tq, tk, heads):
    b, t, _ = proj.shape
    nk = t // tk
    hw = heads * HEAD_DIM
    assert tq % tk == 0 and FOX_HEADS % heads == 0
    return pl.pallas_call(
        functools.partial(_fox_attn_kernel, tq=tq, tk=tk, heads=heads, qscale=HEAD_DIM ** -0.5 * LOG2E),
        out_shape=jax.ShapeDtypeStruct((b, t, FOX_W), BF16),
        grid=(b, FOX_HEADS // heads, t // tq),
        in_specs=[
            pl.BlockSpec((None, tq, hw), lambda bi, h, i: (bi, i, OFF_FQ // hw + h)),
            pl.BlockSpec((None, t, hw), lambda bi, h, i: (bi, 0, OFF_FK // hw + h)),
            pl.BlockSpec((None, t, hw), lambda bi, h, i: (bi, 0, OFF_FV // hw + h)),
            pl.BlockSpec((None, heads, 1, t), lambda bi, h, i: (bi, h, 0, 0)),
        ],
        out_specs=pl.BlockSpec((None, tq, hw), lambda bi, h, i: (bi, i, h)),
        scratch_shapes=[pltpu.VMEM((heads, t, 2 * HEAD_DIM), BF16), pltpu.VMEM((heads, nk, HEAD_DIM, tk), BF16)],
        compiler_params=_params(("parallel", "parallel", "arbitrary")),
        name="fox_attention",
    )(proj, proj, proj, c2.reshape(b, FOX_HEADS, 1, t))


def _rope(x, cos, sin):
    half = ROPE_DIM // 2
    lane = lax.broadcasted_iota(jnp.int32, x.shape, 1)
    swapped = jnp.where(lane < half, pltpu.roll(x, LANES - half, axis=1), pltpu.roll(x, half, axis=1))
    return x * cos + swapped * sin


def _compress(xs_ref, x, w1_ref, w2_ref, pe_ref):
    t = x.shape[0]
    nch = t // CMP_STRIDE
    xs_ref[...] = x
    pe = pe_ref[...]
    first, second = [], []
    for p in range(CMP_STRIDE):
        xp = xs_ref[pl.ds(p, nch, stride=CMP_STRIDE), :]
        first.append((xp + pe[p:p + 1]).astype(BF16))
        second.append((xp + pe[CMP_STRIDE + p:CMP_STRIDE + p + 1]).astype(BF16))
    kw = CMP_STRIDE * HEAD_DIM
    a = _dot(jnp.concatenate(first, axis=1), w1_ref[0:kw, :])
    b = _dot(jnp.concatenate(second, axis=1), w1_ref[kw:2 * kw, :])
    hidden = jax.nn.gelu(a + pltpu.roll(b, nch - 1, axis=0))
    return _dot(hidden.astype(BF16), w2_ref[...])


def _store_transposed_chunks(dst_ref, src_ref):
    nk, _, tile = dst_ref.shape
    for c in range(nk):
        dst_ref[c] = src_ref[c * tile:(c + 1) * tile, :].astype(F32).T.astype(dst_ref.dtype)


def _nsa_prep_kernel(ck_ref, cv_ref, sk_ref, sv_ref, wk_ref, wv_ref, cos_ref, sin_ref,
                     w1k_ref, w2k_ref, pek_ref, w1v_ref, w2v_ref, pev_ref,
                     kc_ref, vc_ref, ske_ref, wkr_ref, svt_ref, wvt_ref, xs_ref):
    cos = cos_ref[...]
    sin = sin_ref[...]
    t = cos.shape[0]
    _store_transposed_chunks(svt_ref, sv_ref)
    _store_transposed_chunks(wvt_ref, wv_ref)
    kc_ref[...] = _compress(xs_ref, _rope(ck_ref[...].astype(F32), cos, sin), w1k_ref, w2k_ref, pek_ref).astype(BF16)
    vc_ref[...] = _compress(xs_ref, cv_ref[...].astype(F32), w1v_ref, w2v_ref, pev_ref).astype(BF16)
    ske_ref[:, 0:HEAD_DIM] = _rope(sk_ref[...].astype(F32), cos, sin).astype(BF16)
    row = lax.broadcasted_iota(jnp.int32, (t, LANES), 0)
    lane = lax.broadcasted_iota(jnp.int32, (t, LANES), 1)
    ske_ref[:, HEAD_DIM:2 * HEAD_DIM] = jnp.where(row // SLC_BLOCK == lane, 1.0, 0.0).astype(BF16)
    wkr_ref[...] = _rope(wk_ref[...].astype(F32), cos, sin).astype(BF16)


def _nsa_prep(proj, cos, sin, w1k, w2k, pek, w1v, w2v, pev, off_kv, slc_tk, win_tk):
    b, t, _ = proj.shape
    g = NSA_KV_HEADS
    nch = t // CMP_STRIDE
    blk = lambda sec: (lambda bi, gi: (bi, 0, (off_kv + sec * NSA_KV_W) // HEAD_DIM + gi))
    full = lambda shape: pl.BlockSpec(shape, lambda bi, gi: (0,) * len(shape))
    per_head = lambda shape: pl.BlockSpec((None, None) + shape, lambda bi, gi: (bi, gi) + (0,) * len(shape))
    return pl.pallas_call(
        _nsa_prep_kernel,
        out_shape=(
            jax.ShapeDtypeStruct((b, g, nch, HEAD_DIM), BF16),
            jax.ShapeDtypeStruct((b, g, nch, HEAD_DIM), BF16),
            jax.ShapeDtypeStruct((b, g, t, 2 * HEAD_DIM), BF16),
            jax.ShapeDtypeStruct((b, g, t, HEAD_DIM), BF16),
            jax.ShapeDtypeStruct((b, g, t // slc_tk, HEAD_DIM, slc_tk), BF16),
            jax.ShapeDtypeStruct((b, g, t // win_tk, HEAD_DIM, win_tk), BF16),
        ),
        grid=(b, g),
        in_specs=[pl.BlockSpec((None, t, HEAD_DIM), blk(sec)) for sec in range(6)] + [
            full((t, LANES)), full((t, LANES)),
            full(w1k.shape), full(w2k.shape), full(pek.shape),
            full(w1v.shape), full(w2v.shape), full(pev.shape),
        ],
        out_specs=(
            per_head((nch, HEAD_DIM)), per_head((nch, HEAD_DIM)),
            per_head((t, 2 * HEAD_DIM)), per_head((t, HEAD_DIM)),
            per_head((t // slc_tk, HEAD_DIM, slc_tk)), per_head((t // win_tk, HEAD_DIM, win_tk)),
        ),
        scratch_shapes=[pltpu.VMEM((t, HEAD_DIM), F32)],
        compiler_params=_params(("parallel", "parallel")),
        name="nsa_prep",
    )(proj, proj, proj, proj, proj, proj, cos, sin, w1k, w2k, pek, w1v, w2v, pev)


def _nsa_cmp_kernel(q_ref, cos_ref, sin_ref, kc_ref, vc_ref, gate_ref, qt_ref, ocmp_ref, sbt_ref, *, tq, scale):
    q0 = pl.program_id(2) * tq
    gates = _sigmoid(gate_ref[...])
    cos = cos_ref[...]
    sin = sin_ref[...]
    kc = kc_ref[...]
    vc = vc_ref[...]
    nch = kc.shape[0]
    n_real = nch - 1
    nsel = (nch * CMP_STRIDE) // SLC_BLOCK
    tpos = q0 + lax.broadcasted_iota(jnp.int32, (tq, nch), 0)
    nidx = lax.broadcasted_iota(jnp.int32, (tq, nch), 1)
    valid = (nidx * CMP_STRIDE + (CMP_LEN - 1) <= tpos) & (nidx < n_real)

    psum = jnp.zeros((tq, nch), F32)
    for j in range(NSA_GROUP):
        sl = slice(j * HEAD_DIM, (j + 1) * HEAD_DIM)
        qf = _rope(q_ref[:, sl].astype(F32), cos, sin)
        qt_ref[j] = (qf * (scale * LOG2E)).T.astype(BF16)
        qj = qf.astype(BF16)
        lg = jnp.where(valid, _dot_nt(qj, kc) * scale, NEG)
        e = jnp.where(valid, jnp.exp(lg - jnp.max(lg, axis=-1, keepdims=True)), 0.0)
        den = jnp.sum(e, axis=-1, keepdims=True)
        inv = 1.0 / jnp.where(den > 0.0, den, 1.0)
        ocmp_ref[:, sl] = _dot((e * (inv * gates[:, j:j + 1])).astype(BF16), vc).astype(ocmp_ref.dtype)
        psum = psum + e * inv

    ratio = SLC_BLOCK // CMP_STRIDE
    jrow = lax.broadcasted_iota(jnp.int32, (nsel, nch), 0)
    ncol = lax.broadcasted_iota(jnp.int32, (nsel, nch), 1)
    lo_n = jrow * ratio - (CMP_LEN - 1) // CMP_STRIDE
    hi_n = jrow * ratio + (SLC_BLOCK - 1) // CMP_STRIDE
    overlap_t = jnp.where((ncol >= lo_n) & (ncol <= hi_n) & (ncol < n_real), 1.0, 0.0).astype(BF16)
    p_hi = psum.astype(BF16)
    r1 = psum - p_hi.astype(F32)
    p_mid = r1.astype(BF16)
    p_lo = (r1 - p_mid.astype(F32)).astype(BF16)
    pslc_t = _dot_nt(overlap_t, p_hi) + _dot_nt(overlap_t, p_mid) + _dot_nt(overlap_t, p_lo)

    blk = lax.broadcasted_iota(jnp.int32, (nsel, tq), 0)
    cur = (q0 + lax.broadcasted_iota(jnp.int32, (nsel, tq), 1)) // SLC_BLOCK
    forced = (blk == 0) | (blk == cur) | (blk == cur - 1)
    score = jnp.where(forced, FORCED_SCORE, jnp.where(blk <= cur, pslc_t, -1.0))

    sub = SUBLANES
    n_groups = nsel // sub
    groups = [score[g * sub:(g + 1) * sub, :] for g in range(n_groups)]
    ranks = [jnp.zeros((sub, tq), F32) for _ in range(n_groups)]
    row_in_group = lax.broadcasted_iota(jnp.int32, (sub, tq), 0)
    for i in range(nsel):
        gi, ri = divmod(i, sub)
        si = score[i:i + 1, :]
        for g in range(n_groups):
            if g > gi:
                beats = si >= groups[g]
            elif g < gi:
                beats = si > groups[g]
            else:
                beats = (si > groups[g]) | ((si == groups[g]) & (row_in_group > ri))
            ranks[g] = ranks[g] + jnp.where(beats, 1.0, 0.0)
    rank = jnp.concatenate(ranks, axis=0)
    k_top = min(SLC_TOPK, nsel)
    bias_t = jnp.where(rank < k_top, 0.0, NEG)
    sbt_ref[...] = jnp.concatenate([bias_t, jnp.zeros((LANES - nsel, tq), F32)], axis=0).astype(BF16)


def _nsa_cmp(proj, cos, sin, kc, vc, gates_cmp, tq):
    b, t, _ = proj.shape
    g = NSA_KV_HEADS
    gw = NSA_GROUP * HEAD_DIM
    nch = kc.shape[2]
    assert (nch * CMP_STRIDE) // SLC_BLOCK <= LANES // 2
    return pl.pallas_call(
        functools.partial(_nsa_cmp_kernel, tq=tq, scale=HEAD_DIM ** -0.5),
        out_shape=(
            jax.ShapeDtypeStruct((b, g, NSA_GROUP, HEAD_DIM, t), BF16),
            jax.ShapeDtypeStruct((b, t, NSA_W), F32),
            jax.ShapeDtypeStruct((b, g, LANES, t), BF16),
        ),
        grid=(b, g, t // tq),
        in_specs=[
            pl.BlockSpec((None, tq, gw), lambda bi, gi, i: (bi, i, OFF_NQ // gw + gi)),
            pl.BlockSpec((tq, LANES), lambda bi, gi, i: (i, 0)),
            pl.BlockSpec((tq, LANES), lambda bi, gi, i: (i, 0)),
            pl.BlockSpec((None, None, nch, HEAD_DIM), lambda bi, gi, i: (bi, gi, 0, 0)),
            pl.BlockSpec((None, None, nch, HEAD_DIM), lambda bi, gi, i: (bi, gi, 0, 0)),
            pl.BlockSpec((None, None, tq, NSA_GROUP), lambda bi, gi, i: (bi, gi, i, 0)),
        ],
        out_specs=(
            pl.BlockSpec((None, None, NSA_GROUP, HEAD_DIM, tq), lambda bi, gi, i: (bi, gi, 0, 0, i)),
            pl.BlockSpec((None, tq, gw), lambda bi, gi, i: (bi, i, gi)),
            pl.BlockSpec((None, None, LANES, tq), lambda bi, gi, i: (bi, gi, 0, i)),
        ),
        compiler_params=_params(("parallel", "parallel", "parallel")),
        name="nsa_cmp_select",
    )(proj, cos, sin, kc, vc, gates_cmp)


def _heads_on_lanes(qt_ref, extra=None):
    parts = []
    for j in range(NSA_GROUP):
        parts.append(qt_ref[j] if extra is None else jnp.concatenate([qt_ref[j], extra], axis=0))
    return jnp.concatenate(parts, axis=1)


def _store_heads(o_ref, o_t, tile):
    for j in range(NSA_GROUP):
        o_ref[:, j * HEAD_DIM:(j + 1) * HEAD_DIM] = o_t[:, j * tile:(j + 1) * tile].T.astype(o_ref.dtype)


def _gate_row(gate_ref):
    g = _sigmoid(gate_ref[...])
    return jnp.concatenate([g[j:j + 1, :] for j in range(NSA_GROUP)], axis=1)


def _nsa_slc_kernel(qt_ref, sbt_ref, ke_ref, vt_ref, gate_ref, o_ref, s_a, s_b, m_scr, l_scr, acc_scr, *, tile):
    i = pl.program_id(2)
    q_t = _heads_on_lanes(qt_ref, sbt_ref[...])
    cols = NSA_GROUP * tile

    def logits(kc):
        k = ke_ref[pl.ds(pl.multiple_of(kc * tile, tile), tile), :]
        return _dot(k, q_t)

    def update(s, kc):
        m, l, acc = _flash_t_step((m_scr[...], l_scr[...], acc_scr[...]), s, vt_ref[kc])
        m_scr[...] = m
        l_scr[...] = l
        acc_scr[...] = acc

    m_scr[...] = jnp.full(m_scr.shape, -jnp.inf, F32)
    l_scr[...] = jnp.zeros(l_scr.shape, F32)
    acc_scr[...] = jnp.zeros(acc_scr.shape, F32)
    s_a[...] = logits(0)

    def pair(jp, carry):
        kc = 2 * jp
        s_b[...] = logits(kc + 1)
        update(s_a[...], kc)
        s_a[...] = logits(kc + 2)
        update(s_b[...], kc + 1)
        return carry

    lax.fori_loop(0, i // 2, pair, 0)
    key = lax.broadcasted_iota(jnp.int32, (tile, cols), 0)
    qry = lax.broadcasted_iota(jnp.int32, (tile, cols), 1) % tile

    @pl.when(i % 2 == 1)
    def _():
        s_b[...] = logits(i)
        update(s_a[...], i - 1)
        update(jnp.where(key <= qry, s_b[...], NEG), i)

    @pl.when(i % 2 == 0)
    def _():
        update(jnp.where(key <= qry, s_a[...], NEG), i)

    _store_heads(o_ref, acc_scr[...] * (_gate_row(gate_ref) / l_scr[...]), tile)


def _nsa_slc(qt, sbt, ske, svt, gates_t, tq):
    b, g, _, _, t = qt.shape
    gw = NSA_GROUP * HEAD_DIM
    nk, tk = svt.shape[2], svt.shape[4]
    assert tq == tk
    cols = NSA_GROUP * tq
    return pl.pallas_call(
        functools.partial(_nsa_slc_kernel, tile=tq),
        out_shape=jax.ShapeDtypeStruct((b, t, NSA_W), F32),
        grid=(b, g, t // tq),
        in_specs=[
            pl.BlockSpec((None, None, NSA_GROUP, HEAD_DIM, tq), lambda bi, gi, i: (bi, gi, 0, 0, i)),
            pl.BlockSpec((None, None, LANES, tq), lambda bi, gi, i: (bi, gi, 0, i)),
            pl.BlockSpec((None, None, t, 2 * HEAD_DIM), lambda bi, gi, i: (bi, gi, 0, 0)),
            pl.BlockSpec((None, None, nk, HEAD_DIM, tk), lambda bi, gi, i: (bi, gi, 0, 0, 0)),
            pl.BlockSpec((None, None, None, NSA_GROUP, tq), lambda bi, gi, i: (bi, 1, gi, 0, i)),
        ],
        out_specs=pl.BlockSpec((None, tq, gw), lambda bi, gi, i: (bi, i, gi)),
        scratch_shapes=[pltpu.VMEM((tk, cols), F32), pltpu.VMEM((tk, cols), F32),
                        pltpu.VMEM((1, cols), F32), pltpu.VMEM((1, cols), F32), pltpu.VMEM((HEAD_DIM, cols), F32)],
        compiler_params=_params(("parallel", "parallel", "arbitrary")),
        name="nsa_selected",
    )(qt, sbt, ske, svt, gates_t)


def _nsa_win_kernel(qt_ref, k_ref, vt_ref, gate_ref, ocmp_ref, oslc_ref, o_ref, *, tile):
    i = pl.program_id(2)
    q_t = _heads_on_lanes(qt_ref)
    cols = NSA_GROUP * tile
    back = WINDOW // tile
    span = (back + 1) * tile
    chunks = [jnp.maximum(i - back + j, 0) for j in range(back + 1)]
    k = jnp.concatenate([k_ref[pl.ds(pl.multiple_of(c * tile, tile), tile), :] for c in chunks], axis=0)
    v_t = jnp.concatenate([vt_ref[c] for c in chunks], axis=1)
    key = lax.broadcasted_iota(jnp.int32, (span, cols), 0) + (i - back) * tile
    qry = lax.broadcasted_iota(jnp.int32, (span, cols), 1) % tile + i * tile
    rel = qry - key
    s = jnp.where((rel >= 0) & (rel < WINDOW) & (key >= 0), _dot(k, q_t), NEG)
    p = jnp.exp2(s - jnp.max(s, axis=0, keepdims=True))
    l = jnp.sum(p, axis=0, keepdims=True)
    o_t = _dot(v_t, p.astype(BF16)) * (_gate_row(gate_ref) / l)
    for j in range(NSA_GROUP):
        sl = slice(j * HEAD_DIM, (j + 1) * HEAD_DIM)
        o_ref[:, sl] = (ocmp_ref[:, sl] + oslc_ref[:, sl] + o_t[:, j * tile:(j + 1) * tile].T).astype(o_ref.dtype)


def _nsa_win(qt, wkr, wvt, gates_t, o_cmp, o_slc, tile):
    b, g, _, _, t = qt.shape
    gw = NSA_GROUP * HEAD_DIM
    nk = t // tile
    assert WINDOW % tile == 0
    rows = pl.BlockSpec((None, tile, gw), lambda bi, gi, i: (bi, i, gi))
    return pl.pallas_call(
        functools.partial(_nsa_win_kernel, tile=tile),
        out_shape=jax.ShapeDtypeStruct((b, t, NSA_W), BF16),
        grid=(b, g, nk),
        in_specs=[
            pl.BlockSpec((None, None, NSA_GROUP, HEAD_DIM, tile), lambda bi, gi, i: (bi, gi, 0, 0, i)),
            pl.BlockSpec((None, None, t, HEAD_DIM), lambda bi, gi, i: (bi, gi, 0, 0)),
            pl.BlockSpec((None, None, nk, HEAD_DIM, tile), lambda bi, gi, i: (bi, gi, 0, 0, 0)),
            pl.BlockSpec((None, None, None, NSA_GROUP, tile), lambda bi, gi, i: (bi, 2, gi, 0, i)),
            rows, rows,
        ],
        out_specs=rows,
        compiler_params=_params(("parallel", "parallel", "parallel")),
        name="nsa_window",
    )(qt, wkr, wvt, gates_t, o_cmp, o_slc)


def _mem_attn_kernel(q_ref, k_ref, v_ref, o_ref, *, scale):
    s = _dot_nt(q_ref[...], k_ref[...]) * scale
    e = jnp.exp(s - jnp.max(s, axis=-1, keepdims=True))
    p = e / jnp.sum(e, axis=-1, keepdims=True)
    o_ref[...] = _dot(p.astype(BF16), v_ref[...]).astype(o_ref.dtype)


def _mem_attention(proj, mkv, off_mq, tq):
    b, t, _ = proj.shape
    m = mkv.shape[1]
    d = MEM_HEAD_DIM
    return pl.pallas_call(
        functools.partial(_mem_attn_kernel, scale=d ** -0.5),
        out_shape=jax.ShapeDtypeStruct((b, t, MEM_W), BF16),
        grid=(b, MEM_HEADS, t // tq),
        in_specs=[
            pl.BlockSpec((None, tq, d), lambda bi, h, i: (bi, i, off_mq // d + h)),
            pl.BlockSpec((None, m, d), lambda bi, h, i: (bi, 0, h)),
            pl.BlockSpec((None, m, d), lambda bi, h, i: (bi, 0, MEM_HEADS + h)),
        ],
        out_specs=pl.BlockSpec((None, tq, d), lambda bi, h, i: (bi, i, h)),
        compiler_params=_params(("parallel", "parallel", "parallel")),
        name="mem_attention",
    )(proj, mkv, mkv)


def _merge_kernel(ofox_ref, onsa_ref, omem_ref, bg0_ref, bg1_ref, bg2_ref, wf_ref, wn_ref, wm_ref, o_ref):
    sig = lambda r: _sigmoid(r[...].astype(F32))
    o_ref[...] = (sig(bg0_ref) * _dot(ofox_ref[...], wf_ref[...])
                  + sig(bg1_ref) * _dot(onsa_ref[...], wn_ref[...])
                  + sig(bg2_ref) * _dot(omem_ref[...], wm_ref[...])).astype(o_ref.dtype)


def _merge(ofox, onsa, omem, proj, wf, wn, wm, tm, tn):
    m, dm = ofox.shape[0], wf.shape[1]
    row = lambda w: pl.BlockSpec((tm, w), lambda j, i: (i, 0))
    gate = lambda br: pl.BlockSpec((tm, tn), lambda j, i: (i, (OFF_BG + br * dm) // tn + j))
    wspec = lambda w: pl.BlockSpec((w.shape[0], tn), lambda j, i: (0, j))
    return pl.pallas_call(
        _merge_kernel,
        out_shape=jax.ShapeDtypeStruct((m, dm), BF16),
        grid=(dm // tn, m // tm),
        in_specs=[row(FOX_W), row(NSA_W), row(MEM_W), gate(0), gate(1), gate(2), wspec(wf), wspec(wn), wspec(wm)],
        out_specs=pl.BlockSpec((tm, tn), lambda j, i: (i, j)),
        compiler_params=_params(("parallel", "parallel")),
        name="branch_merge",
    )(ofox, onsa, omem, proj, proj, proj, wf, wn, wm)


def _out_proj_kernel(m_ref, w_ref, g_ref, h_ref, o_ref):
    o_ref[...] = h_ref[...] + _rms(_dot(m_ref[...], w_ref[...]), g_ref[...])


def _out_proj(merged, w, g, h, tm):
    m, d = h.shape
    return pl.pallas_call(
        _out_proj_kernel,
        out_shape=jax.ShapeDtypeStruct((m, d), F32),
        grid=(m // tm,),
        in_specs=[
            pl.BlockSpec((tm, d), lambda i: (i, 0)),
            pl.BlockSpec((d, d), lambda i: (0, 0)),
            pl.BlockSpec((1, d), lambda i: (0, 0)),
            pl.BlockSpec((tm, d), lambda i: (i, 0)),
        ],
        out_specs=pl.BlockSpec((tm, d), lambda i: (i, 0)),
        compiler_params=_params(("parallel",)),
        name="out_proj",
    )(merged, w, g.reshape(1, d), h)


def _mlp_kernel(h_ref, gpre_ref, w1_ref, w2_ref, gpost_ref, o_ref, u_ref, acc_ref):
    f = pl.program_id(1)

    @pl.when(f == 0)
    def _():
        u_ref[...] = _rms(h_ref[...], gpre_ref[...]).astype(BF16)
        acc_ref[...] = jnp.zeros_like(acc_ref)

    a = jnp.square(jnp.maximum(_dot(u_ref[...], w1_ref[...]), 0.0))
    acc_ref[...] += _dot(a.astype(BF16), w2_ref[...])

    @pl.when(f == pl.num_programs(1) - 1)
    def _():
        o_ref[...] = h_ref[...] + _rms(acc_ref[...], gpost_ref[...])


def _mlp(h, gpre, w1, w2, gpost, tm, tf):
    m, d = h.shape
    dff = w1.shape[1]
    return pl.pallas_call(
        _mlp_kernel,
        out_shape=jax.ShapeDtypeStruct((m, d), F32),
        grid=(m // tm, dff // tf),
        in_specs=[
            pl.BlockSpec((tm, d), lambda i, f: (i, 0)),
            pl.BlockSpec((1, d), lambda i, f: (0, 0)),
            pl.BlockSpec((d, tf), lambda i, f: (0, f)),
            pl.BlockSpec((tf, d), lambda i, f: (f, 0)),
            pl.BlockSpec((1, d), lambda i, f: (0, 0)),
        ],
        out_specs=pl.BlockSpec((tm, d), lambda i, f: (i, 0)),
        scratch_shapes=[pltpu.VMEM((tm, d), BF16), pltpu.VMEM((tm, d), F32)],
        compiler_params=_params(("parallel", "arbitrary")),
        name="mlp",
    )(h, gpre.reshape(1, d), w1, w2, gpost.reshape(1, d))


def _rope_tables(t):
    half = ROPE_DIM // 2
    inv_freq = ROPE_THETA ** (-jnp.arange(half, dtype=F32) / half)
    ang = jnp.arange(t, dtype=jnp.int32).astype(F32)[:, None] * inv_freq[None, :]
    cos, sin = jnp.cos(ang), jnp.sin(ang)
    pad = HEAD_DIM - ROPE_DIM
    cos_t = jnp.concatenate([cos, cos, jnp.ones((t, pad), F32)], axis=1)
    sin_t = jnp.concatenate([-sin, sin, jnp.zeros((t, pad), F32)], axis=1)
    return cos_t, sin_t


def _w_in_kernel(x_ref, o_ref, *, moves):
    for src, dst, width in moves:
        o_ref[:, dst:dst + width] = x_ref[:, src:src + width].astype(o_ref.dtype)


def _split_w_in(w_in, layer, d_model, tr):
    widths = (FOX_W, FOX_W, FOX_W, FOX_HEADS, NSA_W) + (NSA_KV_W,) * 6 + (3 * NSA_HEADS, MEM_W, N_BRANCHES * d_model)
    offs = [int(o) for o in np.concatenate([[0], np.cumsum(widths)])]
    runs = ((0, 3), (4, 5), (13, 14), (5, 11), (12, 13))
    moves, dst = [], 0
    for first, last in runs:
        width = offs[last] - offs[first]
        moves.append((offs[first], dst, width))
        dst += width
    _, rows, cols = w_in.shape
    main = pl.pallas_call(
        functools.partial(_w_in_kernel, moves=tuple(moves)),
        out_shape=jax.ShapeDtypeStruct((rows, dst), BF16),
        grid=(rows // tr,),
        in_specs=[pl.BlockSpec((None, tr, cols), lambda i: (layer, i, 0))],
        out_specs=pl.BlockSpec((tr, dst), lambda i: (i, 0)),
        compiler_params=_params(("parallel",)),
        name="w_in_layout",
    )(w_in)
    w = w_in[layer]
    small = jnp.concatenate([w[:, offs[3]:offs[4]], w[:, offs[11]:offs[12]],
                             jnp.zeros((rows, SMALL_W - FOX_HEADS - 3 * NSA_HEADS), w.dtype)], axis=1).astype(BF16)
    return main, small


def _pick(n, candidates):
    for c in candidates:
        if n % c == 0:
            return c
    raise ValueError(f"no tile for {n}")


def kernel(x, mem, w_in, b_f, w_cmp1_k, w_cmp2_k, pe_cmp_k, w_cmp1_v, w_cmp2_v, pe_cmp_v, w_mem_kv, g_mem, w_up_fox, w_up_nsa, w_up_mem, w_o, g_pre_mix, g_post_mix, g_pre_mlp, g_post_mlp, w_mlp1, w_mlp2):
    b, t, d = x.shape
    depth = w_in.shape[0]
    n = b * t
    mlen = mem.shape[1]
    off_kv = OFF_BG + N_BRANCHES * d
    off_mq = off_kv + 6 * NSA_KV_W
    n_main = off_mq + MEM_W

    cos_t, sin_t = _rope_tables(t)
    tm_proj = _pick(n, (1024, 512, 256))
    tn_proj = _pick(n_main, (1280, 1024, 512, 256, 128))
    attn_tile = _pick(t, (512, 256, 128))
    nsa_tile = _pick(t, (256, 128))
    tm_mem = _pick(b * mlen, (1024, 512, 256))

    h = x.reshape(n, d)
    mem2 = mem.reshape(b * mlen, d)
    for l in range(depth):
        w_main, w_small = _split_w_in(w_in, l, d, 128)
        proj, small = _norm_matmul(h, g_pre_mix[l], w_main, w_small, tm_proj, tn_proj, "in_proj")
        proj = proj.reshape(b, t, n_main)

        ff = small[:, SMALL_FF:SMALL_FF + FOX_HEADS].reshape(b, t, FOX_HEADS).transpose(0, 2, 1)
        c = _fox_cumsum(ff, b_f[l])
        o_fox = _fox_attention(proj, c, 2048, 512, 1)

        kc, vc, ske, wkr, svt, wvt = _nsa_prep(
            proj, cos_t, sin_t,
            w_cmp1_k[l].astype(BF16), w_cmp2_k[l].astype(BF16), pe_cmp_k[l],
            w_cmp1_v[l].astype(BF16), w_cmp2_v[l].astype(BF16), pe_cmp_v[l], off_kv, 2 * nsa_tile, nsa_tile)
        ng = small[:, SMALL_NG:SMALL_NG + 3 * NSA_HEADS].reshape(b, t, 3, NSA_KV_HEADS, NSA_GROUP)
        gates_cmp = ng[:, :, 0].transpose(0, 2, 1, 3)
        gates_t = ng.transpose(0, 2, 3, 4, 1)
        qt, o_cmp, sbt = _nsa_cmp(proj, cos_t, sin_t, kc, vc, gates_cmp, nsa_tile)
        o_slc = _nsa_slc(qt, sbt, ske, svt, gates_t, 2 * nsa_tile)
        o_nsa = _nsa_win(qt, wkr, wvt, gates_t, o_cmp, o_slc, nsa_tile)

        (mkv,) = _norm_matmul(mem2, g_mem[l], w_mem_kv[l].astype(BF16), None, tm_mem, 1024, "mem_kv")
        o_mem = _mem_attention(proj, mkv.reshape(b, mlen, 2 * MEM_W), off_mq, 2 * attn_tile)

        merged = _merge(o_fox.reshape(n, FOX_W), o_nsa.reshape(n, NSA_W), o_mem.reshape(n, MEM_W), proj.reshape(n, n_main),
                        w_up_fox[l].astype(BF16), w_up_nsa[l].astype(BF16), w_up_mem[l].astype(BF16), 512, 1024)
        h = _out_proj(merged, w_o[l].astype(BF16), g_post_mix[l], h, 512)
        w1 = _layer_bf16(w_mlp1, l, w_mlp1.shape[1] // 8)
        w2 = _layer_bf16(w_mlp2, l, w_mlp2.shape[1] // 8)
        h = _mlp(h, g_pre_mlp[l], w1, w2, g_post_mlp[l], 512, 1024)
    return h.reshape(b, t, d)
```

```python
import functools

import numpy as np
import jax
import jax.numpy as jnp
from jax import lax
from jax.experimental import pallas as pl
from jax.experimental.pallas import tpu as pltpu

F32 = jnp.float32
BF16 = jnp.bfloat16

HEAD_DIM = 128
FOX_HEADS = 8
NSA_HEADS = 8
NSA_KV_HEADS = 2
NSA_GROUP = NSA_HEADS // NSA_KV_HEADS
MEM_HEADS = 4
MEM_HEAD_DIM = 256
N_BRANCHES = 3
CMP_LEN = 32
CMP_STRIDE = 16
SLC_BLOCK = 64
SLC_TOPK = 16
WINDOW = 512
ROPE_THETA = 500000.0
ROPE_DIM = HEAD_DIM // 4
RMS_EPS = 1e-6
NEG = -1e30
FORCED_SCORE = 1e6
LOG2E = 1.4426950408889634

LANES = 128
SUBLANES = 8
VMEM_LIMIT = 56 * 1024 * 1024

FOX_W = FOX_HEADS * HEAD_DIM
NSA_W = NSA_HEADS * HEAD_DIM
NSA_KV_W = NSA_KV_HEADS * HEAD_DIM
MEM_W = MEM_HEADS * MEM_HEAD_DIM
OFF_FQ = 0
OFF_FK = OFF_FQ + FOX_W
OFF_FV = OFF_FK + FOX_W
OFF_NQ = OFF_FV + FOX_W
OFF_BG = OFF_NQ + NSA_W
SMALL_W = LANES
SMALL_FF = 0
SMALL_NG = FOX_HEADS


def _params(sem):
    return pltpu.CompilerParams(dimension_semantics=sem, vmem_limit_bytes=VMEM_LIMIT)


def _rms(x, g):
    return (x * lax.rsqrt(jnp.mean(x * x, axis=-1, keepdims=True) + RMS_EPS)) * g


def _sigmoid(x):
    return 0.5 * jnp.tanh(0.5 * x) + 0.5


def _dot(a, b):
    return jnp.dot(a, b, preferred_element_type=F32)


def _dot_nt(a, b):
    return lax.dot_general(a, b, (((1,), (1,)), ((), ())), preferred_element_type=F32)


def _cast_kernel(x_ref, o_ref):
    o_ref[...] = x_ref[...].astype(o_ref.dtype)


def _layer_bf16(w, layer, tr):
    _, r, c = w.shape
    return pl.pallas_call(
        _cast_kernel,
        out_shape=jax.ShapeDtypeStruct((r, c), BF16),
        grid=(r // tr,),
        in_specs=[pl.BlockSpec((None, tr, c), lambda i: (layer, i, 0))],
        out_specs=pl.BlockSpec((tr, c), lambda i: (i, 0)),
        compiler_params=_params(("parallel",)),
        name="weight_bf16",
    )(w)


def _norm_matmul_kernel(x_ref, g_ref, w_ref, *rest, narrow):
    if narrow:
        ws_ref, o_ref, os_ref, u_ref = rest
    else:
        o_ref, u_ref = rest

    @pl.when(pl.program_id(1) == 0)
    def _():
        u_ref[...] = _rms(x_ref[...], g_ref[...]).astype(BF16)
        if narrow:
            os_ref[...] = _dot(u_ref[...], ws_ref[...])

    o_ref[...] = _dot(u_ref[...], w_ref[...]).astype(o_ref.dtype)


def _norm_matmul(x, g, w, ws, tm, tn, name):
    m, d = x.shape
    n = w.shape[1]
    narrow = ws is not None
    in_specs = [
        pl.BlockSpec((tm, d), lambda i, j: (i, 0)),
        pl.BlockSpec((1, d), lambda i, j: (0, 0)),
        pl.BlockSpec((d, tn), lambda i, j: (0, j)),
    ]
    out_shape = [jax.ShapeDtypeStruct((m, n), BF16)]
    out_specs = [pl.BlockSpec((tm, tn), lambda i, j: (i, j))]
    args = [x, g.reshape(1, d), w]
    if narrow:
        in_specs.append(pl.BlockSpec((d, SMALL_W), lambda i, j: (0, 0)))
        out_shape.append(jax.ShapeDtypeStruct((m, SMALL_W), F32))
        out_specs.append(pl.BlockSpec((tm, SMALL_W), lambda i, j: (i, 0)))
        args.append(ws)
    return pl.pallas_call(
        functools.partial(_norm_matmul_kernel, narrow=narrow),
        out_shape=tuple(out_shape),
        grid=(m // tm, n // tn),
        in_specs=in_specs,
        out_specs=tuple(out_specs),
        scratch_shapes=[pltpu.VMEM((tm, d), BF16)],
        compiler_params=_params(("parallel", "arbitrary")),
        name=name,
    )(*args)


def _fox_cumsum_kernel(bf_ref, ff_ref, c_ref):
    x = ff_ref[...] + bf_ref[pl.program_id(1)]
    lf = jnp.minimum(x, 0.0) - jnp.log1p(jnp.exp(-jnp.abs(x)))
    rows, lanes = lf.shape
    lane = lax.broadcasted_iota(jnp.int32, lf.shape, 1)
    row = lax.broadcasted_iota(jnp.int32, lf.shape, 0)
    y = lf
    s = 1
    while s < lanes:
        y = y + jnp.where(lane >= s, pltpu.roll(y, s, axis=1), 0.0)
        s *= 2
    tot = jnp.broadcast_to(y[:, lanes - 1:lanes], lf.shape)
    z = tot
    s = 1
    while s < rows:
        z = z + jnp.where(row >= s, pltpu.roll(z, s, axis=0), 0.0)
        s *= 2
    c_ref[...] = (y + (z - tot)) * LOG2E


def _fox_cumsum(ff, b_f):
    b, h, t = ff.shape
    rows = t // LANES
    out = pl.pallas_call(
        _fox_cumsum_kernel,
        out_shape=jax.ShapeDtypeStruct((b, h, rows, LANES), F32),
        grid=(b, h),
        in_specs=[
            pl.BlockSpec(memory_space=pltpu.SMEM),
            pl.BlockSpec((None, None, rows, LANES), lambda i, j: (i, j, 0, 0)),
        ],
        out_specs=pl.BlockSpec((None, None, rows, LANES), lambda i, j: (i, j, 0, 0)),
        compiler_params=_params(("parallel", "parallel")),
        name="fox_cumsum",
    )(b_f, ff.reshape(b, h, rows, LANES))
    return out.reshape(b, h, t)


def _flash_t_step(carry, s, v_t):
    m, l, acc = carry
    m_new = jnp.maximum(m, jnp.max(s, axis=0, keepdims=True))
    alpha = jnp.exp2(m - m_new)
    p = jnp.exp2(s - m_new)
    l = alpha * l + jnp.sum(p, axis=0, keepdims=True)
    acc = alpha * acc + _dot(v_t, p.astype(BF16))
    return m_new, l, acc


def _flash_t_init(cols, d):
    return (jnp.full((1, cols), -jnp.inf, F32), jnp.zeros((1, cols), F32), jnp.zeros((d, cols), F32))


FOX_C_TERMS = 3


def _fox_attn_kernel(q_ref, k_ref, v_ref, c_ref, o_ref, ke_scr, vt_scr, *, tq, tk, heads, qscale):
    i = pl.program_id(2)
    nk = vt_scr.shape[1]

    @pl.when(i == 0)
    def _():
        row = lax.broadcasted_iota(jnp.int32, (HEAD_DIM, tk), 0)
        for hh in range(heads):
            hs = slice(hh * HEAD_DIM, (hh + 1) * HEAD_DIM)
            for kc in range(nk):
                rows = slice(kc * tk, (kc + 1) * tk)
                vt_scr[hh, kc] = v_ref[rows, hs].astype(F32).T.astype(BF16)
                ke_scr[hh, rows, 0:HEAD_DIM] = k_ref[rows, hs]
                c = c_ref[hh, :, rows]
                terms = jnp.zeros((HEAD_DIM, tk), F32)
                for n in range(FOX_C_TERMS):
                    part = c.astype(BF16).astype(F32)
                    terms = jnp.where(row == n, part, terms)
                    c = c - part
                ke_scr[hh, rows, HEAD_DIM:2 * HEAD_DIM] = terms.T.astype(BF16)

    minus_ones = jnp.where(lax.broadcasted_iota(jnp.int32, (HEAD_DIM, tq), 0) < FOX_C_TERMS, -1.0, 0.0).astype(BF16)
    q_ts = []
    for hh in range(heads):
        q_t = (q_ref[:, hh * HEAD_DIM:(hh + 1) * HEAD_DIM].astype(F32) * qscale).T.astype(BF16)
        q_ts.append(jnp.concatenate([q_t, minus_ones], axis=0))
    n_full = i * (tq // tk)

    def body(kc, states):
        start = pl.multiple_of(kc * tk, tk)
        return tuple(_flash_t_step(states[hh], _dot(ke_scr[hh, pl.ds(start, tk), :], q_ts[hh]), vt_scr[hh, kc])
                     for hh in range(heads))

    states = lax.fori_loop(0, n_full, body, tuple(_flash_t_init(tq, HEAD_DIM) for _ in range(heads)))

    for d in range(tq // tk):
        lo = d * tk
        key = lax.broadcasted_iota(jnp.int32, (tk, tq - lo), 0)
        col = lax.broadcasted_iota(jnp.int32, (tk, tq - lo), 1)
        start = pl.multiple_of((n_full + d) * tk, tk)
        new_states = []
        for hh in range(heads):
            m, l, acc = states[hh]
            s = jnp.where(key <= col, _dot(ke_scr[hh, pl.ds(start, tk), :], q_ts[hh][:, lo:]), NEG)
            upd = _flash_t_step((m[:, lo:], l[:, lo:], acc[:, lo:]), s, vt_scr[hh, n_full + d])
            if lo:
                upd = tuple(jnp.concatenate([old[:, :lo], new], axis=1) for old, new in zip((m, l, acc), upd))
            new_states.append(upd)
        states = new_states

    for hh in range(heads):
        _, l, acc = states[hh]
        o_ref[:, hh * HEAD_DIM:(hh + 1) * HEAD_DIM] = (acc / l).T.astype(o_ref.dtype)


def _fox_attention(proj, c2, tq, tk, heads):
    b, t, _ = proj.shape
    nk = t // tk
    hw = heads * HEAD_DIM
    assert tq % tk == 0 and FOX_HEADS % heads == 0
    return pl.pallas_call(
        functools.partial(_fox_attn_kernel, tq=tq, tk=tk, heads=heads, qscale=HEAD_DIM ** -0.5 * LOG2E),
        out_shape=jax.ShapeDtypeStruct((b, t, FOX_W), BF16),
        grid=(b, FOX_HEADS // heads, t // tq),
        in_specs=[
            pl.BlockSpec((None, tq, hw), lambda bi, h, i: (bi, i, OFF_FQ // hw + h)),
            pl.BlockSpec((None, t, hw), lambda bi, h, i: (bi, 0, OFF_FK // hw + h)),
            pl.BlockSpec((None, t, hw), lambda bi, h, i: (bi, 0, OFF_FV // hw + h)),
            pl.BlockSpec((None, heads, 1, t), lambda bi, h, i: (bi, h, 0, 0)),
        ],
        out_specs=pl.BlockSpec((None, tq, hw), lambda bi, h, i: (bi, i, h)),
        scratch_shapes=[pltpu.VMEM((heads, t, 2 * HEAD_DIM), BF16), pltpu.VMEM((heads, nk, HEAD_DIM, tk), BF16)],
        compiler_params=_params(("parallel", "parallel", "arbitrary")),
        name="fox_attention",
    )(proj, proj, proj, c2.reshape(b, FOX_HEADS, 1, t))


def _rope(x, cos, sin):
    half = ROPE_DIM // 2
    lane = lax.broadcasted_iota(jnp.int32, x.shape, 1)
    swapped = jnp.where(lane < half, pltpu.roll(x, LANES - half, axis=1), pltpu.roll(x, half, axis=1))
    return x * cos + swapped * sin


def _compress(xs_ref, x, w1_ref, w2_ref, pe_ref):
    t = x.shape[0]
    nch = t // CMP_STRIDE
    xs_ref[...] = x
    pe = pe_ref[...]
    first, second = [], []
    for p in range(CMP_STRIDE):
        xp = xs_ref[pl.ds(p, nch, stride=CMP_STRIDE), :]
        first.append((xp + pe[p:p + 1]).astype(BF16))
        second.append((xp + pe[CMP_STRIDE + p:CMP_STRIDE + p + 1]).astype(BF16))
    kw = CMP_STRIDE * HEAD_DIM
    a = _dot(jnp.concatenate(first, axis=1), w1_ref[0:kw, :])
    b = _dot(jnp.concatenate(second, axis=1), w1_ref[kw:2 * kw, :])
    hidden = jax.nn.gelu(a + pltpu.roll(b, nch - 1, axis=0))
    return _dot(hidden.astype(BF16), w2_ref[...])


def _store_transposed_chunks(dst_ref, src_ref):
    nk, _, tile = dst_ref.shape
    for c in range(nk):
        dst_ref[c] = src_ref[c * tile:(c + 1) * tile, :].astype(F32).T.astype(dst_ref.dtype)


def _nsa_prep_kernel(ck_ref, cv_ref, sk_ref, sv_ref, wk_ref, wv_ref, cos_ref, sin_ref,
                     w1k_ref, w2k_ref, pek_ref, w1v_ref, w2v_ref, pev_ref,
                     kc_ref, vc_ref, ske_ref, wkr_ref, svt_ref, wvt_ref, xs_ref):
    cos = cos_ref[...]
    sin = sin_ref[...]
    t = cos.shape[0]
    _store_transposed_chunks(svt_ref, sv_ref)
    _store_transposed_chunks(wvt_ref, wv_ref)
    kc_ref[...] = _compress(xs_ref, _rope(ck_ref[...].astype(F32), cos, sin), w1k_ref, w2k_ref, pek_ref).astype(BF16)
    vc_ref[...] = _compress(xs_ref, cv_ref[...].astype(F32), w1v_ref, w2v_ref, pev_ref).astype(BF16)
    ske_ref[:, 0:HEAD_DIM] = _rope(sk_ref[...].astype(F32), cos, sin).astype(BF16)
    row = lax.broadcasted_iota(jnp.int32, (t, LANES), 0)
    lane = lax.broadcasted_iota(jnp.int32, (t, LANES), 1)
    ske_ref[:, HEAD_DIM:2 * HEAD_DIM] = jnp.where(row // SLC_BLOCK == lane, 1.0, 0.0).astype(BF16)
    wkr_ref[...] = _rope(wk_ref[...].astype(F32), cos, sin).astype(BF16)


def _nsa_prep(proj, cos, sin, w1k, w2k, pek, w1v, w2v, pev, off_kv, slc_tk, win_tk):
    b, t, _ = proj.shape
    g = NSA_KV_HEADS
    nch = t // CMP_STRIDE
    blk = lambda sec: (lambda bi, gi: (bi, 0, (off_kv + sec * NSA_KV_W) // HEAD_DIM + gi))
    full = lambda shape: pl.BlockSpec(shape, lambda bi, gi: (0,) * len(shape))
    per_head = lambda shape: pl.BlockSpec((None, None) + shape, lambda bi, gi: (bi, gi) + (0,) * len(shape))
    return pl.pallas_call(
        _nsa_prep_kernel,
        out_shape=(
            jax.ShapeDtypeStruct((b, g, nch, HEAD_DIM), BF16),
            jax.ShapeDtypeStruct((b, g, nch, HEAD_DIM), BF16),
            jax.ShapeDtypeStruct((b, g, t, 2 * HEAD_DIM), BF16),
            jax.ShapeDtypeStruct((b, g, t, HEAD_DIM), BF16),
            jax.ShapeDtypeStruct((b, g, t // slc_tk, HEAD_DIM, slc_tk), BF16),
            jax.ShapeDtypeStruct((b, g, t // win_tk, HEAD_DIM, win_tk), BF16),
        ),
        grid=(b, g),
        in_specs=[pl.BlockSpec((None, t, HEAD_DIM), blk(sec)) for sec in range(6)] + [
            full((t, LANES)), full((t, LANES)),
            full(w1k.shape), full(w2k.shape), full(pek.shape),
            full(w1v.shape), full(w2v.shape), full(pev.shape),
        ],
        out_specs=(
            per_head((nch, HEAD_DIM)), per_head((nch, HEAD_DIM)),
            per_head((t, 2 * HEAD_DIM)), per_head((t, HEAD_DIM)),
            per_head((t // slc_tk, HEAD_DIM, slc_tk)), per_head((t // win_tk, HEAD_DIM, win_tk)),
        ),
        scratch_shapes=[pltpu.VMEM((t, HEAD_DIM), F32)],
        compiler_params=_params(("parallel", "parallel")),
        name="nsa_prep",
    )(proj, proj, proj, proj, proj, proj, cos, sin, w1k, w2k, pek, w1v, w2v, pev)


def _nsa_cmp_kernel(q_ref, cos_ref, sin_ref, kc_ref, vc_ref, gate_ref, qt_ref, ocmp_ref, sbt_ref, *, tq, scale):
    q0 = pl.program_id(2) * tq
    gates = _sigmoid(gate_ref[...])
    cos = cos_ref[...]
    sin = sin_ref[...]
    kc = kc_ref[...]
    vc = vc_ref[...]
    nch = kc.shape[0]
    n_real = nch - 1
    nsel = (nch * CMP_STRIDE) // SLC_BLOCK
    tpos = q0 + lax.broadcasted_iota(jnp.int32, (tq, nch), 0)
    nidx = lax.broadcasted_iota(jnp.int32, (tq, nch), 1)
    valid = (nidx * CMP_STRIDE + (CMP_LEN - 1) <= tpos) & (nidx < n_real)

    psum = jnp.zeros((tq, nch), F32)
    for j in range(NSA_GROUP):
        sl = slice(j * HEAD_DIM, (j + 1) * HEAD_DIM)
        qf = _rope(q_ref[:, sl].astype(F32), cos, sin)
        qt_ref[j] = (qf * (scale * LOG2E)).T.astype(BF16)
        qj = qf.astype(BF16)
        lg = jnp.where(valid, _dot_nt(qj, kc) * scale, NEG)
        e = jnp.where(valid, jnp.exp(lg - jnp.max(lg, axis=-1, keepdims=True)), 0.0)
        den = jnp.sum(e, axis=-1, keepdims=True)
        inv = 1.0 / jnp.where(den > 0.0, den, 1.0)
        ocmp_ref[:, sl] = _dot((e * (inv * gates[:, j:j + 1])).astype(BF16), vc).astype(ocmp_ref.dtype)
        psum = psum + e * inv

    ratio = SLC_BLOCK // CMP_STRIDE
    jrow = lax.broadcasted_iota(jnp.int32, (nsel, nch), 0)
    ncol = lax.broadcasted_iota(jnp.int32, (nsel, nch), 1)
    lo_n = jrow * ratio - (CMP_LEN - 1) // CMP_STRIDE
    hi_n = jrow * ratio + (SLC_BLOCK - 1) // CMP_STRIDE
    overlap_t = jnp.where((ncol >= lo_n) & (ncol <= hi_n) & (ncol < n_real), 1.0, 0.0).astype(BF16)
    p_hi = psum.astype(BF16)
    r1 = psum - p_hi.astype(F32)
    p_mid = r1.astype(BF16)
    p_lo = (r1 - p_mid.astype(F32)).astype(BF16)
    pslc_t = _dot_nt(overlap_t, p_hi) + _dot_nt(overlap_t, p_mid) + _dot_nt(overlap_t, p_lo)

    blk = lax.broadcasted_iota(jnp.int32, (nsel, tq), 0)
    cur = (q0 + lax.broadcasted_iota(jnp.int32, (nsel, tq), 1)) // SLC_BLOCK
    forced = (blk == 0) | (blk == cur) | (blk == cur - 1)
    score = jnp.where(forced, FORCED_SCORE, jnp.where(blk <= cur, pslc_t, -1.0))

    sub = SUBLANES
    n_groups = nsel // sub
    groups = [score[g * sub:(g + 1) * sub, :] for g in range(n_groups)]
    ranks = [jnp.zeros((sub, tq), F32) for _ in range(n_groups)]
    row_in_group = lax.broadcasted_iota(jnp.int32, (sub, tq), 0)
    for i in range(nsel):
        gi, ri = divmod(i, sub)
        si = score[i:i + 1, :]
        for g in range(n_groups):
            if g > gi:
                beats = si >= groups[g]
            elif g < gi:
                beats = si > groups[g]
            else:
                beats = (si > groups[g]) | ((si == groups[g]) & (row_in_group > ri))
            ranks[g] = ranks[g] + jnp.where(beats, 1.0, 0.0)
    rank = jnp.concatenate(ranks, axis=0)
    k_top = min(SLC_TOPK, nsel)
    bias_t = jnp.where(rank < k_top, 0.0, NEG)
    sbt_ref[...] = jnp.concatenate([bias_t, jnp.zeros((LANES - nsel, tq), F32)], axis=0).astype(BF16)


def _nsa_cmp(proj, cos, sin, kc, vc, gates_cmp, tq):
    b, t, _ = proj.shape
    g = NSA_KV_HEADS
    gw = NSA_GROUP * HEAD_DIM
    nch = kc.shape[2]
    assert (nch * CMP_STRIDE) // SLC_BLOCK <= LANES // 2
    return pl.pallas_call(
        functools.partial(_nsa_cmp_kernel, tq=tq, scale=HEAD_DIM ** -0.5),
        out_shape=(
            jax.ShapeDtypeStruct((b, g, NSA_GROUP, HEAD_DIM, t), BF16),
            jax.ShapeDtypeStruct((b, t, NSA_W), F32),
            jax.ShapeDtypeStruct((b, g, LANES, t), BF16),
        ),
        grid=(b, g, t // tq),
        in_specs=[
            pl.BlockSpec((None, tq, gw), lambda bi, gi, i: (bi, i, OFF_NQ // gw + gi)),
            pl.BlockSpec((tq, LANES), lambda bi, gi, i: (i, 0)),
            pl.BlockSpec((tq, LANES), lambda bi, gi, i: (i, 0)),
            pl.BlockSpec((None, None, nch, HEAD_DIM), lambda bi, gi, i: (bi, gi, 0, 0)),
            pl.BlockSpec((None, None, nch, HEAD_DIM), lambda bi, gi, i: (bi, gi, 0, 0)),
            pl.BlockSpec((None, None, tq, NSA_GROUP), lambda bi, gi, i: (bi, gi, i, 0)),
        ],
        out_specs=(
            pl.BlockSpec((None, None, NSA_GROUP, HEAD_DIM, tq), lambda bi, gi, i: (bi, gi, 0, 0, i)),
            pl.BlockSpec((None, tq, gw), lambda bi, gi, i: (bi, i, gi)),
            pl.BlockSpec((None, None, LANES, tq), lambda bi, gi, i: (bi, gi, 0, i)),
        ),
        compiler_params=_params(("parallel", "parallel", "parallel")),
        name="nsa_cmp_select",
    )(proj, cos, sin, kc, vc, gates_cmp)


def _heads_on_lanes(qt_ref, extra=None):
    parts = []
    for j in range(NSA_GROUP):
        parts.append(qt_ref[j] if extra is None else jnp.concatenate([qt_ref[j], extra], axis=0))
    return jnp.concatenate(parts, axis=1)


def _store_heads(o_ref, o_t, tile):
    for j in range(NSA_GROUP):
        o_ref[:, j * HEAD_DIM:(j + 1) * HEAD_DIM] = o_t[:, j * tile:(j + 1) * tile].T.astype(o_ref.dtype)


def _gate_row(gate_ref):
    g = _sigmoid(gate_ref[...])
    return jnp.concatenate([g[j:j + 1, :] for j in range(NSA_GROUP)], axis=1)


def _nsa_slc_kernel(qt_ref, sbt_ref, ke_ref, vt_ref, gate_ref, o_ref, s_a, s_b, m_scr, l_scr, acc_scr, *, tile):
    i = pl.program_id(2)
    q_t = _heads_on_lanes(qt_ref, sbt_ref[...])
    cols = NSA_GROUP * tile

    def logits(kc):
        k = ke_ref[pl.ds(pl.multiple_of(kc * tile, tile), tile), :]
        return _dot(k, q_t)

    def update(s, kc):
        m, l, acc = _flash_t_step((m_scr[...], l_scr[...], acc_scr[...]), s, vt_ref[kc])
        m_scr[...] = m
        l_scr[...] = l
        acc_scr[...] = acc

    m_scr[...] = jnp.full(m_scr.shape, -jnp.inf, F32)
    l_scr[...] = jnp.zeros(l_scr.shape, F32)
    acc_scr[...] = jnp.zeros(acc_scr.shape, F32)
    s_a[...] = logits(0)

    def pair(jp, carry):
        kc = 2 * jp
        s_b[...] = logits(kc + 1)
        update(s_a[...], kc)
        s_a[...] = logits(kc + 2)
        update(s_b[...], kc + 1)
        return carry

    lax.fori_loop(0, i // 2, pair, 0)
    key = lax.broadcasted_iota(jnp.int32, (tile, cols), 0)
    qry = lax.broadcasted_iota(jnp.int32, (tile, cols), 1) % tile

    @pl.when(i % 2 == 1)
    def _():
        s_b[...] = logits(i)
        update(s_a[...], i - 1)
        update(jnp.where(key <= qry, s_b[...], NEG), i)

    @pl.when(i % 2 == 0)
    def _():
        update(jnp.where(key <= qry, s_a[...], NEG), i)

    _store_heads(o_ref, acc_scr[...] * (_gate_row(gate_ref) / l_scr[...]), tile)


def _nsa_slc(qt, sbt, ske, svt, gates_t, tq):
    b, g, _, _, t = qt.shape
    gw = NSA_GROUP * HEAD_DIM
    nk, tk = svt.shape[2], svt.shape[4]
    assert tq == tk
    cols = NSA_GROUP * tq
    return pl.pallas_call(
        functools.partial(_nsa_slc_kernel, tile=tq),
        out_shape=jax.ShapeDtypeStruct((b, t, NSA_W), F32),
        grid=(b, g, t // tq),
        in_specs=[
            pl.BlockSpec((None, None, NSA_GROUP, HEAD_DIM, tq), lambda bi, gi, i: (bi, gi, 0, 0, i)),
            pl.BlockSpec((None, None, LANES, tq), lambda bi, gi, i: (bi, gi, 0, i)),
            pl.BlockSpec((None, None, t, 2 * HEAD_DIM), lambda bi, gi, i: (bi, gi, 0, 0)),
            pl.BlockSpec((None, None, nk, HEAD_DIM, tk), lambda bi, gi, i: (bi, gi, 0, 0, 0)),
            pl.BlockSpec((None, None, None, NSA_GROUP, tq), lambda bi, gi, i: (bi, 1, gi, 0, i)),
        ],
        out_specs=pl.BlockSpec((None, tq, gw), lambda bi, gi, i: (bi, i, gi)),
        scratch_shapes=[pltpu.VMEM((tk, cols), F32), pltpu.VMEM((tk, cols), F32),
                        pltpu.VMEM((1, cols), F32), pltpu.VMEM((1, cols), F32), pltpu.VMEM((HEAD_DIM, cols), F32)],
        compiler_params=_params(("parallel", "parallel", "arbitrary")),
        name="nsa_selected",
    )(qt, sbt, ske, svt, gates_t)


def _nsa_win_kernel(qt_ref, k_ref, vt_ref, gate_ref, ocmp_ref, oslc_ref, o_ref, *, tile):
    i = pl.program_id(2)
    q_t = _heads_on_lanes(qt_ref)
    cols = NSA_GROUP * tile
    back = WINDOW // tile
    chunks = [jnp.maximum(i - back + j, 0) for j in range(back + 1)]
    v_t = jnp.concatenate([vt_ref[c] for c in chunks], axis=1)
    r = lax.broadcasted_iota(jnp.int32, (tile, cols), 0)
    c = lax.broadcasted_iota(jnp.int32, (tile, cols), 1) % tile
    blocks = []
    for j in range(back + 1):
        sj = _dot(k_ref[pl.ds(pl.multiple_of(chunks[j] * tile, tile), tile), :], q_t)
        if j == 0:
            sj = jnp.where(r > c, sj, NEG)
        if j == back:
            sj = jnp.where(c >= r, sj, NEG)
        else:
            sj = sj + jnp.where(i - back + j >= 0, 0.0, NEG)
        blocks.append(sj)
    s = jnp.concatenate(blocks, axis=0)
    p = jnp.exp2(s - jnp.max(s, axis=0, keepdims=True))
    l = jnp.sum(p, axis=0, keepdims=True)
    o_t = _dot(v_t, p.astype(BF16)) * (_gate_row(gate_ref) / l)
    for j in range(NSA_GROUP):
        sl = slice(j * HEAD_DIM, (j + 1) * HEAD_DIM)
        o_ref[:, sl] = (ocmp_ref[:, sl] + oslc_ref[:, sl] + o_t[:, j * tile:(j + 1) * tile].T).astype(o_ref.dtype)


def _nsa_win(qt, wkr, wvt, gates_t, o_cmp, o_slc, tile):
    b, g, _, _, t = qt.shape
    gw = NSA_GROUP * HEAD_DIM
    nk = t // tile
    assert WINDOW % tile == 0 and WINDOW >= tile
    rows = pl.BlockSpec((None, tile, gw), lambda bi, gi, i: (bi, i, gi))
    return pl.pallas_call(
        functools.partial(_nsa_win_kernel, tile=tile),
        out_shape=jax.ShapeDtypeStruct((b, t, NSA_W), BF16),
        grid=(b, g, nk),
        in_specs=[
            pl.BlockSpec((None, None, NSA_GROUP, HEAD_DIM, tile), lambda bi, gi, i: (bi, gi, 0, 0, i)),
            pl.BlockSpec((None, None, t, HEAD_DIM), lambda bi, gi, i: (bi, gi, 0, 0)),
            pl.BlockSpec((None, None, nk, HEAD_DIM, tile), lambda bi, gi, i: (bi, gi, 0, 0, 0)),
            pl.BlockSpec((None, None, None, NSA_GROUP, tile), lambda bi, gi, i: (bi, 2, gi, 0, i)),
            rows, rows,
        ],
        out_specs=rows,
        compiler_params=_params(("parallel", "parallel", "parallel")),
        name="nsa_window",
    )(qt, wkr, wvt, gates_t, o_cmp, o_slc)


def _mem_attn_kernel(q_ref, k_ref, v_ref, o_ref, *, scale):
    s = _dot_nt(q_ref[...], k_ref[...]) * scale
    e = jnp.exp(s - jnp.max(s, axis=-1, keepdims=True))
    p = e / jnp.sum(e, axis=-1, keepdims=True)
    o_ref[...] = _dot(p.astype(BF16), v_ref[...]).astype(o_ref.dtype)


def _mem_attention(proj, mkv, off_mq, tq):
    b, t, _ = proj.shape
    m = mkv.shape[1]
    d = MEM_HEAD_DIM
    return pl.pallas_call(
        functools.partial(_mem_attn_kernel, scale=d ** -0.5),
        out_shape=jax.ShapeDtypeStruct((b, t, MEM_W), BF16),
        grid=(b, MEM_HEADS, t // tq),
        in_specs=[
            pl.BlockSpec((None, tq, d), lambda bi, h, i: (bi, i, off_mq // d + h)),
            pl.BlockSpec((None, m, d), lambda bi, h, i: (bi, 0, h)),
            pl.BlockSpec((None, m, d), lambda bi, h, i: (bi, 0, MEM_HEADS + h)),
        ],
        out_specs=pl.BlockSpec((None, tq, d), lambda bi, h, i: (bi, i, h)),
        compiler_params=_params(("parallel", "parallel", "parallel")),
        name="mem_attention",
    )(proj, mkv, mkv)


def _merge_kernel(ofox_ref, onsa_ref, omem_ref, bg0_ref, bg1_ref, bg2_ref, wf_ref, wn_ref, wm_ref, o_ref):
    sig = lambda r: _sigmoid(r[...].astype(F32))
    o_ref[...] = (sig(bg0_ref) * _dot(ofox_ref[...], wf_ref[...])
                  + sig(bg1_ref) * _dot(onsa_ref[...], wn_ref[...])
                  + sig(bg2_ref) * _dot(omem_ref[...], wm_ref[...])).astype(o_ref.dtype)


def _merge(ofox, onsa, omem, proj, wf, wn, wm, tm, tn):
    m, dm = ofox.shape[0], wf.shape[1]
    row = lambda w: pl.BlockSpec((tm, w), lambda j, i: (i, 0))
    gate = lambda br: pl.BlockSpec((tm, tn), lambda j, i: (i, (OFF_BG + br * dm) // tn + j))
    wspec = lambda w: pl.BlockSpec((w.shape[0], tn), lambda j, i: (0, j))
    return pl.pallas_call(
        _merge_kernel,
        out_shape=jax.ShapeDtypeStruct((m, dm), BF16),
        grid=(dm // tn, m // tm),
        in_specs=[row(FOX_W), row(NSA_W), row(MEM_W), gate(0), gate(1), gate(2), wspec(wf), wspec(wn), wspec(wm)],
        out_specs=pl.BlockSpec((tm, tn), lambda j, i: (i, j)),
        compiler_params=_params(("parallel", "parallel")),
        name="branch_merge",
    )(ofox, onsa, omem, proj, proj, proj, wf, wn, wm)


def _out_proj_kernel(m_ref, w_ref, g_ref, h_ref, o_ref):
    o_ref[...] = h_ref[...] + _rms(_dot(m_ref[...], w_ref[...]), g_ref[...])


def _out_proj(merged, w, g, h, tm):
    m, d = h.shape
    return pl.pallas_call(
        _out_proj_kernel,
        out_shape=jax.ShapeDtypeStruct((m, d), F32),
        grid=(m // tm,),
        in_specs=[
            pl.BlockSpec((tm, d), lambda i: (i, 0)),
            pl.BlockSpec((d, d), lambda i: (0, 0)),
            pl.BlockSpec((1, d), lambda i: (0, 0)),
            pl.BlockSpec((tm, d), lambda i: (i, 0)),
        ],
        out_specs=pl.BlockSpec((tm, d), lambda i: (i, 0)),
        compiler_params=_params(("parallel",)),
        name="out_proj",
    )(merged, w, g.reshape(1, d), h)


def _mlp_kernel(h_ref, gpre_ref, w1_ref, w2_ref, gpost_ref, o_ref, u_ref, acc_ref):
    f = pl.program_id(1)

    @pl.when(f == 0)
    def _():
        u_ref[...] = _rms(h_ref[...], gpre_ref[...]).astype(BF16)
        acc_ref[...] = jnp.zeros_like(acc_ref)

    a = jnp.square(jnp.maximum(_dot(u_ref[...], w1_ref[...]), 0.0))
    acc_ref[...] += _dot(a.astype(BF16), w2_ref[...])

    @pl.when(f == pl.num_programs(1) - 1)
    def _():
        o_ref[...] = h_ref[...] + _rms(acc_ref[...], gpost_ref[...])


def _mlp(h, gpre, w1, w2, gpost, tm, tf):
    m, d = h.shape
    dff = w1.shape[1]
    return pl.pallas_call(
        _mlp_kernel,
        out_shape=jax.ShapeDtypeStruct((m, d), F32),
        grid=(m // tm, dff // tf),
        in_specs=[
            pl.BlockSpec((tm, d), lambda i, f: (i, 0)),
            pl.BlockSpec((1, d), lambda i, f: (0, 0)),
            pl.BlockSpec((d, tf), lambda i, f: (0, f)),
            pl.BlockSpec((tf, d), lambda i, f: (f, 0)),
            pl.BlockSpec((1, d), lambda i, f: (0, 0)),
        ],
        out_specs=pl.BlockSpec((tm, d), lambda i, f: (i, 0)),
        scratch_shapes=[pltpu.VMEM((tm, d), BF16), pltpu.VMEM((tm, d), F32)],
        compiler_params=_params(("parallel", "arbitrary")),
        name="mlp",
    )(h, gpre.reshape(1, d), w1, w2, gpost.reshape(1, d))


def _rope_tables(t):
    half = ROPE_DIM // 2
    inv_freq = ROPE_THETA ** (-jnp.arange(half, dtype=F32) / half)
    ang = jnp.arange(t, dtype=jnp.int32).astype(F32)[:, None] * inv_freq[None, :]
    cos, sin = jnp.cos(ang), jnp.sin(ang)
    pad = HEAD_DIM - ROPE_DIM
    cos_t = jnp.concatenate([cos, cos, jnp.ones((t, pad), F32)], axis=1)
    sin_t = jnp.concatenate([-sin, sin, jnp.zeros((t, pad), F32)], axis=1)
    return cos_t, sin_t


def _w_in_kernel(x_ref, *rest):
    o_ref = rest[-1]
    o_ref[...] = x_ref[0].T.astype(o_ref.dtype)


def _rows_kernel(x_ref, o_ref):
    o_ref[...] = x_ref[0]


def _split_w_in(w_in_t, layer, d_model, tc):
    widths = (FOX_W, FOX_W, FOX_W, FOX_HEADS, NSA_W) + (NSA_KV_W,) * 6 + (3 * NSA_HEADS, MEM_W, N_BRANCHES * d_model)
    offs = [int(o) for o in np.concatenate([[0], np.cumsum(widths)])]
    runs = ((0, 3), (4, 5), (13, 14), (5, 11), (12, 13))
    n_main = sum(offs[last] - offs[first] for first, last in runs)
    _, _, d = w_in_t.shape
    main, dst = None, 0
    for first, last in runs:
        src, width = offs[first], offs[last] - offs[first]
        assert width % tc == 0 and dst % tc == 0
        in_specs = [pl.BlockSpec((pl.Element(1), pl.Element(tc), pl.Element(d)),
                                 lambda j, src=src: (layer, pl.multiple_of(src + j * tc, SUBLANES), 0))]
        args = [w_in_t]
        if main is not None:
            in_specs.append(pl.BlockSpec(memory_space=pl.ANY))
            args.append(main)
        main = pl.pallas_call(
            _w_in_kernel,
            out_shape=jax.ShapeDtypeStruct((d, n_main), BF16),
            grid=(width // tc,),
            in_specs=in_specs,
            out_specs=pl.BlockSpec((d, tc), lambda j, dst=dst: (0, dst // tc + j)),
            input_output_aliases={} if len(args) == 1 else {1: 0},
            compiler_params=_params(("parallel",)),
            name="w_in_layout",
        )(*args)
        dst += width
    def take_rows(start, count):
        return pl.pallas_call(
            _rows_kernel,
            out_shape=jax.ShapeDtypeStruct((count, d), w_in_t.dtype),
            grid=(1,),
            in_specs=[pl.BlockSpec((pl.Element(1), pl.Element(count), pl.Element(d)), lambda j: (layer, start, 0))],
            out_specs=pl.BlockSpec((count, d), lambda j: (0, 0)),
            name="w_in_rows",
        )(w_in_t)

    narrow = jnp.concatenate([take_rows(offs[3], widths[3]), take_rows(offs[11], widths[11]),
                              jnp.zeros((SMALL_W - widths[3] - widths[11], d), w_in_t.dtype)], axis=0)
    return main, narrow.T.astype(BF16)


def _pick(n, candidates):
    for c in candidates:
        if n % c == 0:
            return c
    raise ValueError(f"no tile for {n}")


def kernel(x, mem, w_in, b_f, w_cmp1_k, w_cmp2_k, pe_cmp_k, w_cmp1_v, w_cmp2_v, pe_cmp_v, w_mem_kv, g_mem, w_up_fox, w_up_nsa, w_up_mem, w_o, g_pre_mix, g_post_mix, g_pre_mlp, g_post_mlp, w_mlp1, w_mlp2):
    b, t, d = x.shape
    depth = w_in.shape[0]
    n = b * t
    mlen = mem.shape[1]
    off_kv = OFF_BG + N_BRANCHES * d
    off_mq = off_kv + 6 * NSA_KV_W
    n_main = off_mq + MEM_W

    cos_t, sin_t = _rope_tables(t)
    tm_proj = _pick(n, (1024, 512, 256))
    tn_proj = _pick(n_main, (1280, 1024, 512, 256, 128))
    attn_tile = _pick(t, (512, 256, 128))
    nsa_tile = _pick(t, (256, 128))
    tm_mem = _pick(b * mlen, (1024, 512, 256))

    h = x.reshape(n, d)
    mem2 = mem.reshape(b * mlen, d)
    w_in_t = jnp.swapaxes(w_in, 1, 2)
    for l in range(depth):
        w_main, w_small = _split_w_in(w_in_t, l, d, 512)
        proj, small = _norm_matmul(h, g_pre_mix[l], w_main, w_small, tm_proj, tn_proj, "in_proj")
        proj = proj.reshape(b, t, n_main)

        ff = small[:, SMALL_FF:SMALL_FF + FOX_HEADS].reshape(b, t, FOX_HEADS).transpose(0, 2, 1)
        c = _fox_cumsum(ff, b_f[l])
        o_fox = _fox_attention(proj, c, 2048, 512, 1)

        kc, vc, ske, wkr, svt, wvt = _nsa_prep(
            proj, cos_t, sin_t,
            w_cmp1_k[l].astype(BF16), w_cmp2_k[l].astype(BF16), pe_cmp_k[l],
            w_cmp1_v[l].astype(BF16), w_cmp2_v[l].astype(BF16), pe_cmp_v[l], off_kv, 2 * nsa_tile, nsa_tile)
        ng = small[:, SMALL_NG:SMALL_NG + 3 * NSA_HEADS].reshape(b, t, 3, NSA_KV_HEADS, NSA_GROUP)
        gates_cmp = ng[:, :, 0].transpose(0, 2, 1, 3)
        gates_t = ng.transpose(0, 2, 3, 4, 1)
        qt, o_cmp, sbt = _nsa_cmp(proj, cos_t, sin_t, kc, vc, gates_cmp, nsa_tile)
        o_slc = _nsa_slc(qt, sbt, ske, svt, gates_t, 2 * nsa_tile)
        o_nsa = _nsa_win(qt, wkr, wvt, gates_t, o_cmp, o_slc, nsa_tile)

        half = lambda w: _layer_bf16(w, l, w.shape[1] // 2)
        (mkv,) = _norm_matmul(mem2, g_mem[l], half(w_mem_kv), None, tm_mem, 1024, "mem_kv")
        o_mem = _mem_attention(proj, mkv.reshape(b, mlen, 2 * MEM_W), off_mq, 2 * attn_tile)

        merged = _merge(o_fox.reshape(n, FOX_W), o_nsa.reshape(n, NSA_W), o_mem.reshape(n, MEM_W), proj.reshape(n, n_main),
                        half(w_up_fox), half(w_up_nsa), half(w_up_mem), 512, 1024)
        h = _out_proj(merged, half(w_o), g_post_mix[l], h, 512)
        w1 = _layer_bf16(w_mlp1, l, w_mlp1.shape[1] // 8)
        w2 = _layer_bf16(w_mlp2, l, w_mlp2.shape[1] // 8)
        h = _mlp(h, g_pre_mlp[l], w1, w2, g_post_mlp[l], 512, 1024)
    return h.reshape(b, t, d)
```

```python
import functools

import numpy as np
import jax
import jax.numpy as jnp
from jax import lax
from jax.experimental import pallas as pl
from jax.experimental.pallas import tpu as pltpu

F32 = jnp.float32
BF16 = jnp.bfloat16

HEAD_DIM = 128
FOX_HEADS = 8
NSA_HEADS = 8
NSA_KV_HEADS = 2
NSA_GROUP = NSA_HEADS // NSA_KV_HEADS
MEM_HEADS = 4
MEM_HEAD_DIM = 256
N_BRANCHES = 3
CMP_LEN = 32
CMP_STRIDE = 16
SLC_BLOCK = 64
SLC_TOPK = 16
WINDOW = 512
ROPE_THETA = 500000.0
ROPE_DIM = HEAD_DIM // 4
RMS_EPS = 1e-6
NEG = -1e30
FORCED_SCORE = 1e6
LOG2E = 1.4426950408889634

LANES = 128
SUBLANES = 8
VMEM_LIMIT = 56 * 1024 * 1024

FOX_W = FOX_HEADS * HEAD_DIM
NSA_W = NSA_HEADS * HEAD_DIM
NSA_KV_W = NSA_KV_HEADS * HEAD_DIM
MEM_W = MEM_HEADS * MEM_HEAD_DIM
OFF_FQ = 0
OFF_FK = OFF_FQ + FOX_W
OFF_FV = OFF_FK + FOX_W
OFF_NQ = OFF_FV + FOX_W
OFF_BG = OFF_NQ + NSA_W
SMALL_W = LANES
SMALL_FF = 0
SMALL_NG = FOX_HEADS


def _params(sem):
    return pltpu.CompilerParams(dimension_semantics=sem, vmem_limit_bytes=VMEM_LIMIT)


def _rms(x, g):
    return (x * lax.rsqrt(jnp.mean(x * x, axis=-1, keepdims=True) + RMS_EPS)) * g


def _sigmoid(x):
    return 0.5 * jnp.tanh(0.5 * x) + 0.5


def _dot(a, b):
    return jnp.dot(a, b, preferred_element_type=F32)


def _dot_nt(a, b):
    return lax.dot_general(a, b, (((1,), (1,)), ((), ())), preferred_element_type=F32)


def _cast_kernel(x_ref, o_ref):
    o_ref[...] = x_ref[...].astype(o_ref.dtype)


def _layer_bf16(w, layer, tr):
    _, r, c = w.shape
    return pl.pallas_call(
        _cast_kernel,
        out_shape=jax.ShapeDtypeStruct((r, c), BF16),
        grid=(r // tr,),
        in_specs=[pl.BlockSpec((None, tr, c), lambda i: (layer, i, 0))],
        out_specs=pl.BlockSpec((tr, c), lambda i: (i, 0)),
        compiler_params=_params(("parallel",)),
        name="weight_bf16",
    )(w)


def _norm_matmul_kernel(x_ref, g_ref, w_ref, *rest, narrow):
    if narrow:
        ws_ref, o_ref, os_ref, u_ref = rest
    else:
        o_ref, u_ref = rest

    @pl.when(pl.program_id(1) == 0)
    def _():
        u_ref[...] = _rms(x_ref[...], g_ref[...]).astype(BF16)
        if narrow:
            os_ref[...] = _dot(u_ref[...], ws_ref[...])

    o_ref[...] = _dot(u_ref[...], w_ref[...]).astype(o_ref.dtype)


def _norm_matmul(x, g, w, ws, tm, tn, name):
    m, d = x.shape
    n = w.shape[1]
    narrow = ws is not None
    in_specs = [
        pl.BlockSpec((tm, d), lambda i, j: (i, 0)),
        pl.BlockSpec((1, d), lambda i, j: (0, 0)),
        pl.BlockSpec((d, tn), lambda i, j: (0, j)),
    ]
    out_shape = [jax.ShapeDtypeStruct((m, n), BF16)]
    out_specs = [pl.BlockSpec((tm, tn), lambda i, j: (i, j))]
    args = [x, g.reshape(1, d), w]
    if narrow:
        in_specs.append(pl.BlockSpec((d, SMALL_W), lambda i, j: (0, 0)))
        out_shape.append(jax.ShapeDtypeStruct((m, SMALL_W), F32))
        out_specs.append(pl.BlockSpec((tm, SMALL_W), lambda i, j: (i, 0)))
        args.append(ws)
    return pl.pallas_call(
        functools.partial(_norm_matmul_kernel, narrow=narrow),
        out_shape=tuple(out_shape),
        grid=(m // tm, n // tn),
        in_specs=in_specs,
        out_specs=tuple(out_specs),
        scratch_shapes=[pltpu.VMEM((tm, d), BF16)],
        compiler_params=_params(("parallel", "arbitrary")),
        name=name,
    )(*args)


def _fox_cumsum_kernel(bf_ref, ff_ref, c_ref):
    x = ff_ref[...] + bf_ref[pl.program_id(1)]
    lf = jnp.minimum(x, 0.0) - jnp.log1p(jnp.exp(-jnp.abs(x)))
    rows, lanes = lf.shape
    lane = lax.broadcasted_iota(jnp.int32, lf.shape, 1)
    row = lax.broadcasted_iota(jnp.int32, lf.shape, 0)
    y = lf
    s = 1
    while s < lanes:
        y = y + jnp.where(lane >= s, pltpu.roll(y, s, axis=1), 0.0)
        s *= 2
    tot = jnp.broadcast_to(y[:, lanes - 1:lanes], lf.shape)
    z = tot
    s = 1
    while s < rows:
        z = z + jnp.where(row >= s, pltpu.roll(z, s, axis=0), 0.0)
        s *= 2
    c_ref[...] = (y + (z - tot)) * LOG2E


def _fox_cumsum(ff, b_f):
    b, h, t = ff.shape
    rows = t // LANES
    out = pl.pallas_call(
        _fox_cumsum_kernel,
        out_shape=jax.ShapeDtypeStruct((b, h, rows, LANES), F32),
        grid=(b, h),
        in_specs=[
            pl.BlockSpec(memory_space=pltpu.SMEM),
            pl.BlockSpec((None, None, rows, LANES), lambda i, j: (i, j, 0, 0)),
        ],
        out_specs=pl.BlockSpec((None, None, rows, LANES), lambda i, j: (i, j, 0, 0)),
        compiler_params=_params(("parallel", "parallel")),
        name="fox_cumsum",
    )(b_f, ff.reshape(b, h, rows, LANES))
    return out.reshape(b, h, t)


def _flash_t_step(carry, s, v_t):
    m, l, acc = carry
    m_new = jnp.maximum(m, jnp.max(s, axis=0, keepdims=True))
    alpha = jnp.exp2(m - m_new)
    p = jnp.exp2(s - m_new)
    l = alpha * l + jnp.sum(p, axis=0, keepdims=True)
    acc = alpha * acc + _dot(v_t, p.astype(BF16))
    return m_new, l, acc


def _flash_t_init(cols, d):
    return (jnp.full((1, cols), -jnp.inf, F32), jnp.zeros((1, cols), F32), jnp.zeros((d, cols), F32))


FOX_C_TERMS = 3


def _fox_attn_kernel(q_ref, k_ref, v_ref, c_ref, o_ref, ke_scr, vt_scr, s_a, s_b, m_scr, l_scr, acc_scr,
                     *, tq, tk, n_tiles, qscale):
    i = pl.program_id(2)
    nk = vt_scr.shape[0]
    per_tile = tq // tk

    @pl.when(i == 0)
    def _():
        row = lax.broadcasted_iota(jnp.int32, (HEAD_DIM, tk), 0)
        for kc in range(nk):
            rows = slice(kc * tk, (kc + 1) * tk)
            vt_scr[kc] = v_ref[rows, :].astype(F32).T.astype(BF16)
            ke_scr[rows, 0:HEAD_DIM] = k_ref[rows, :]
            c = c_ref[:, rows]
            terms = jnp.zeros((HEAD_DIM, tk), F32)
            for n in range(FOX_C_TERMS):
                part = c.astype(BF16).astype(F32)
                terms = jnp.where(row == n, part, terms)
                c = c - part
            ke_scr[rows, HEAD_DIM:2 * HEAD_DIM] = terms.T.astype(BF16)

    minus_ones = jnp.where(lax.broadcasted_iota(jnp.int32, (HEAD_DIM, tq), 0) < FOX_C_TERMS, -1.0, 0.0).astype(BF16)
    q_t = jnp.concatenate([(q_ref[...].astype(F32) * qscale).T.astype(BF16), minus_ones], axis=0)

    def logits(kc, lo):
        return _dot(ke_scr[kc * tk:(kc + 1) * tk, :], q_t[:, lo:])

    def run(tile):
        chunks = [(kc, 0, False) for kc in range(tile * per_tile)]
        chunks += [(tile * per_tile + d, d * tk, True) for d in range(per_tile)]
        m_scr[...] = jnp.full(m_scr.shape, -jnp.inf, F32)
        l_scr[...] = jnp.zeros(l_scr.shape, F32)
        acc_scr[...] = jnp.zeros(acc_scr.shape, F32)
        bufs = (s_a, s_b)
        bufs[0][:, chunks[0][1]:] = logits(chunks[0][0], chunks[0][1])
        for idx, (kc, lo, masked) in enumerate(chunks):
            if idx + 1 < len(chunks):
                kn, ln, _ = chunks[idx + 1]
                bufs[(idx + 1) % 2][:, ln:] = logits(kn, ln)
            s = bufs[idx % 2][:, lo:]
            if masked:
                key = lax.broadcasted_iota(jnp.int32, (tk, tq - lo), 0)
                col = lax.broadcasted_iota(jnp.int32, (tk, tq - lo), 1)
                s = jnp.where(key <= col, s, NEG)
            m, l, acc = _flash_t_step((m_scr[:, lo:], l_scr[:, lo:], acc_scr[:, lo:]), s, vt_scr[kc])
            m_scr[:, lo:] = m
            l_scr[:, lo:] = l
            acc_scr[:, lo:] = acc
        o_ref[...] = (acc_scr[...] / l_scr[...]).T.astype(o_ref.dtype)

    for tile in range(n_tiles):
        pl.when(i == tile)(functools.partial(run, tile))


def _fox_attention(proj, c2, tq, tk):
    b, t, _ = proj.shape
    nk = t // tk
    n_tiles = t // tq
    assert tq % tk == 0
    qb, kb, vb = OFF_FQ // HEAD_DIM, OFF_FK // HEAD_DIM, OFF_FV // HEAD_DIM
    return pl.pallas_call(
        functools.partial(_fox_attn_kernel, tq=tq, tk=tk, n_tiles=n_tiles, qscale=HEAD_DIM ** -0.5 * LOG2E),
        out_shape=jax.ShapeDtypeStruct((b, t, FOX_W), BF16),
        grid=(b, FOX_HEADS, n_tiles),
        in_specs=[
            pl.BlockSpec((None, tq, HEAD_DIM), lambda bi, h, i: (bi, i, qb + h)),
            pl.BlockSpec((None, t, HEAD_DIM), lambda bi, h, i: (bi, 0, kb + h)),
            pl.BlockSpec((None, t, HEAD_DIM), lambda bi, h, i: (bi, 0, vb + h)),
            pl.BlockSpec((None, None, 1, t), lambda bi, h, i: (bi, h, 0, 0)),
        ],
        out_specs=pl.BlockSpec((None, tq, HEAD_DIM), lambda bi, h, i: (bi, i, h)),
        scratch_shapes=[
            pltpu.VMEM((t, 2 * HEAD_DIM), BF16), pltpu.VMEM((nk, HEAD_DIM, tk), BF16),
            pltpu.VMEM((tk, tq), F32), pltpu.VMEM((tk, tq), F32),
            pltpu.VMEM((1, tq), F32), pltpu.VMEM((1, tq), F32), pltpu.VMEM((HEAD_DIM, tq), F32),
        ],
        compiler_params=_params(("parallel", "parallel", "arbitrary")),
        name="fox_attention",
    )(proj, proj, proj, c2.reshape(b, FOX_HEADS, 1, t))


def _rope(x, cos, sin):
    half = ROPE_DIM // 2
    lane = lax.broadcasted_iota(jnp.int32, x.shape, 1)
    swapped = jnp.where(lane < half, pltpu.roll(x, LANES - half, axis=1), pltpu.roll(x, half, axis=1))
    return x * cos + swapped * sin


def _compress(xs_ref, x, w1_ref, w2_ref, pe_ref):
    t = x.shape[0]
    nch = t // CMP_STRIDE
    xs_ref[...] = x
    pe = pe_ref[...]
    first, second = [], []
    for p in range(CMP_STRIDE):
        xp = xs_ref[pl.ds(p, nch, stride=CMP_STRIDE), :]
        first.append((xp + pe[p:p + 1]).astype(BF16))
        second.append((xp + pe[CMP_STRIDE + p:CMP_STRIDE + p + 1]).astype(BF16))
    kw = CMP_STRIDE * HEAD_DIM
    a = _dot(jnp.concatenate(first, axis=1), w1_ref[0:kw, :])
    b = _dot(jnp.concatenate(second, axis=1), w1_ref[kw:2 * kw, :])
    hidden = jax.nn.gelu(a + pltpu.roll(b, nch - 1, axis=0))
    return _dot(hidden.astype(BF16), w2_ref[...])


def _store_transposed_chunks(dst_ref, src_ref):
    nk, _, tile = dst_ref.shape
    for c in range(nk):
        dst_ref[c] = src_ref[c * tile:(c + 1) * tile, :].astype(F32).T.astype(dst_ref.dtype)


def _nsa_prep_kernel(ck_ref, cv_ref, sk_ref, sv_ref, wk_ref, wv_ref, cos_ref, sin_ref,
                     w1k_ref, w2k_ref, pek_ref, w1v_ref, w2v_ref, pev_ref,
                     kc_ref, vc_ref, ske_ref, wkr_ref, svt_ref, wvt_ref, xs_ref):
    cos = cos_ref[...]
    sin = sin_ref[...]
    t = cos.shape[0]
    _store_transposed_chunks(svt_ref, sv_ref)
    _store_transposed_chunks(wvt_ref, wv_ref)
    kc_ref[...] = _compress(xs_ref, _rope(ck_ref[...].astype(F32), cos, sin), w1k_ref, w2k_ref, pek_ref).astype(BF16)
    vc_ref[...] = _compress(xs_ref, cv_ref[...].astype(F32), w1v_ref, w2v_ref, pev_ref).astype(BF16)
    ske_ref[:, 0:HEAD_DIM] = _rope(sk_ref[...].astype(F32), cos, sin).astype(BF16)
    row = lax.broadcasted_iota(jnp.int32, (t, LANES), 0)
    lane = lax.broadcasted_iota(jnp.int32, (t, LANES), 1)
    ske_ref[:, HEAD_DIM:2 * HEAD_DIM] = jnp.where(row // SLC_BLOCK == lane, 1.0, 0.0).astype(BF16)
    wkr_ref[...] = _rope(wk_ref[...].astype(F32), cos, sin).astype(BF16)


def _nsa_prep(proj, cos, sin, w1k, w2k, pek, w1v, w2v, pev, off_kv, slc_tk, win_tk):
    b, t, _ = proj.shape
    g = NSA_KV_HEADS
    nch = t // CMP_STRIDE
    blk = lambda sec: (lambda bi, gi: (bi, 0, (off_kv + sec * NSA_KV_W) // HEAD_DIM + gi))
    full = lambda shape: pl.BlockSpec(shape, lambda bi, gi: (0,) * len(shape))
    per_head = lambda shape: pl.BlockSpec((None, None) + shape, lambda bi, gi: (bi, gi) + (0,) * len(shape))
    return pl.pallas_call(
        _nsa_prep_kernel,
        out_shape=(
            jax.ShapeDtypeStruct((b, g, nch, HEAD_DIM), BF16),
            jax.ShapeDtypeStruct((b, g, nch, HEAD_DIM), BF16),
            jax.ShapeDtypeStruct((b, g, t, 2 * HEAD_DIM), BF16),
            jax.ShapeDtypeStruct((b, g, t, HEAD_DIM), BF16),
            jax.ShapeDtypeStruct((b, g, t // slc_tk, HEAD_DIM, slc_tk), BF16),
            jax.ShapeDtypeStruct((b, g, t // win_tk, HEAD_DIM, win_tk), BF16),
        ),
        grid=(b, g),
        in_specs=[pl.BlockSpec((None, t, HEAD_DIM), blk(sec)) for sec in range(6)] + [
            full((t, LANES)), full((t, LANES)),
            full(w1k.shape), full(w2k.shape), full(pek.shape),
            full(w1v.shape), full(w2v.shape), full(pev.shape),
        ],
        out_specs=(
            per_head((nch, HEAD_DIM)), per_head((nch, HEAD_DIM)),
            per_head((t, 2 * HEAD_DIM)), per_head((t, HEAD_DIM)),
            per_head((t // slc_tk, HEAD_DIM, slc_tk)), per_head((t // win_tk, HEAD_DIM, win_tk)),
        ),
        scratch_shapes=[pltpu.VMEM((t, HEAD_DIM), F32)],
        compiler_params=_params(("parallel", "parallel")),
        name="nsa_prep",
    )(proj, proj, proj, proj, proj, proj, cos, sin, w1k, w2k, pek, w1v, w2v, pev)


def _nsa_cmp_kernel(q_ref, cos_ref, sin_ref, kc_ref, vc_ref, gate_ref, qt_ref, ocmp_ref, sbt_ref, *, tq, scale):
    q0 = pl.program_id(2) * tq
    gates = _sigmoid(gate_ref[...])
    cos = cos_ref[...]
    sin = sin_ref[...]
    kc = kc_ref[...]
    vc = vc_ref[...]
    nch = kc.shape[0]
    n_real = nch - 1
    nsel = (nch * CMP_STRIDE) // SLC_BLOCK
    tpos = q0 + lax.broadcasted_iota(jnp.int32, (tq, nch), 0)
    nidx = lax.broadcasted_iota(jnp.int32, (tq, nch), 1)
    valid = (nidx * CMP_STRIDE + (CMP_LEN - 1) <= tpos) & (nidx < n_real)

    psum = jnp.zeros((tq, nch), F32)
    for j in range(NSA_GROUP):
        sl = slice(j * HEAD_DIM, (j + 1) * HEAD_DIM)
        qf = _rope(q_ref[:, sl].astype(F32), cos, sin)
        qt_ref[j] = (qf * (scale * LOG2E)).T.astype(BF16)
        qj = qf.astype(BF16)
        lg = jnp.where(valid, _dot_nt(qj, kc) * scale, NEG)
        e = jnp.where(valid, jnp.exp(lg - jnp.max(lg, axis=-1, keepdims=True)), 0.0)
        den = jnp.sum(e, axis=-1, keepdims=True)
        inv = 1.0 / jnp.where(den > 0.0, den, 1.0)
        ocmp_ref[:, sl] = _dot((e * (inv * gates[:, j:j + 1])).astype(BF16), vc).astype(ocmp_ref.dtype)
        psum = psum + e * inv

    ratio = SLC_BLOCK // CMP_STRIDE
    jrow = lax.broadcasted_iota(jnp.int32, (nsel, nch), 0)
    ncol = lax.broadcasted_iota(jnp.int32, (nsel, nch), 1)
    lo_n = jrow * ratio - (CMP_LEN - 1) // CMP_STRIDE
    hi_n = jrow * ratio + (SLC_BLOCK - 1) // CMP_STRIDE
    overlap_t = jnp.where((ncol >= lo_n) & (ncol <= hi_n) & (ncol < n_real), 1.0, 0.0).astype(BF16)
    p_hi = psum.astype(BF16)
    r1 = psum - p_hi.astype(F32)
    p_mid = r1.astype(BF16)
    p_lo = (r1 - p_mid.astype(F32)).astype(BF16)
    pslc_t = _dot_nt(overlap_t, p_hi) + _dot_nt(overlap_t, p_mid) + _dot_nt(overlap_t, p_lo)

    blk = lax.broadcasted_iota(jnp.int32, (nsel, tq), 0)
    cur = (q0 + lax.broadcasted_iota(jnp.int32, (nsel, tq), 1)) // SLC_BLOCK
    forced = (blk == 0) | (blk == cur) | (blk == cur - 1)
    score = jnp.where(forced, FORCED_SCORE, jnp.where(blk <= cur, pslc_t, -1.0))

    sub = SUBLANES
    n_groups = nsel // sub
    groups = [score[g * sub:(g + 1) * sub, :] for g in range(n_groups)]
    ranks = [jnp.zeros((sub, tq), F32) for _ in range(n_groups)]
    row_in_group = lax.broadcasted_iota(jnp.int32, (sub, tq), 0)
    for i in range(nsel):
        gi, ri = divmod(i, sub)
        si = score[i:i + 1, :]
        for g in range(n_groups):
            if g > gi:
                beats = si >= groups[g]
            elif g < gi:
                beats = si > groups[g]
            else:
                beats = (si > groups[g]) | ((si == groups[g]) & (row_in_group > ri))
            ranks[g] = ranks[g] + jnp.where(beats, 1.0, 0.0)
    rank = jnp.concatenate(ranks, axis=0)
    k_top = min(SLC_TOPK, nsel)
    bias_t = jnp.where(rank < k_top, 0.0, NEG)
    sbt_ref[...] = jnp.concatenate([bias_t, jnp.zeros((LANES - nsel, tq), F32)], axis=0).astype(BF16)


def _nsa_cmp(proj, cos, sin, kc, vc, gates_cmp, tq):
    b, t, _ = proj.shape
    g = NSA_KV_HEADS
    gw = NSA_GROUP * HEAD_DIM
    nch = kc.shape[2]
    assert (nch * CMP_STRIDE) // SLC_BLOCK <= LANES // 2
    return pl.pallas_call(
        functools.partial(_nsa_cmp_kernel, tq=tq, scale=HEAD_DIM ** -0.5),
        out_shape=(
            jax.ShapeDtypeStruct((b, g, NSA_GROUP, HEAD_DIM, t), BF16),
            jax.ShapeDtypeStruct((b, t, NSA_W), F32),
            jax.ShapeDtypeStruct((b, g, LANES, t), BF16),
        ),
        grid=(b, g, t // tq),
        in_specs=[
            pl.BlockSpec((None, tq, gw), lambda bi, gi, i: (bi, i, OFF_NQ // gw + gi)),
            pl.BlockSpec((tq, LANES), lambda bi, gi, i: (i, 0)),
            pl.BlockSpec((tq, LANES), lambda bi, gi, i: (i, 0)),
            pl.BlockSpec((None, None, nch, HEAD_DIM), lambda bi, gi, i: (bi, gi, 0, 0)),
            pl.BlockSpec((None, None, nch, HEAD_DIM), lambda bi, gi, i: (bi, gi, 0, 0)),
            pl.BlockSpec((None, None, tq, NSA_GROUP), lambda bi, gi, i: (bi, gi, i, 0)),
        ],
        out_specs=(
            pl.BlockSpec((None, None, NSA_GROUP, HEAD_DIM, tq), lambda bi, gi, i: (bi, gi, 0, 0, i)),
            pl.BlockSpec((None, tq, gw), lambda bi, gi, i: (bi, i, gi)),
            pl.BlockSpec((None, None, LANES, tq), lambda bi, gi, i: (bi, gi, 0, i)),
        ),
        compiler_params=_params(("parallel", "parallel", "parallel")),
        name="nsa_cmp_select",
    )(proj, cos, sin, kc, vc, gates_cmp)


def _heads_on_lanes(qt_ref, extra=None):
    parts = []
    for j in range(NSA_GROUP):
        parts.append(qt_ref[j] if extra is None else jnp.concatenate([qt_ref[j], extra], axis=0))
    return jnp.concatenate(parts, axis=1)


def _store_heads(o_ref, o_t, tile):
    for j in range(NSA_GROUP):
        o_ref[:, j * HEAD_DIM:(j + 1) * HEAD_DIM] = o_t[:, j * tile:(j + 1) * tile].T.astype(o_ref.dtype)


def _gate_row(gate_ref):
    g = _sigmoid(gate_ref[...])
    return jnp.concatenate([g[j:j + 1, :] for j in range(NSA_GROUP)], axis=1)


def _nsa_slc_kernel(qt_ref, sbt_ref, ke_ref, vt_ref, gate_ref, o_ref, s_a, s_b, m_scr, l_scr, acc_scr, *, tile):
    i = pl.program_id(2)
    q_t = _heads_on_lanes(qt_ref, sbt_ref[...])
    cols = NSA_GROUP * tile

    def logits(kc):
        k = ke_ref[pl.ds(pl.multiple_of(kc * tile, tile), tile), :]
        return _dot(k, q_t)

    def update(s, kc):
        m, l, acc = _flash_t_step((m_scr[...], l_scr[...], acc_scr[...]), s, vt_ref[kc])
        m_scr[...] = m
        l_scr[...] = l
        acc_scr[...] = acc

    m_scr[...] = jnp.full(m_scr.shape, -jnp.inf, F32)
    l_scr[...] = jnp.zeros(l_scr.shape, F32)
    acc_scr[...] = jnp.zeros(acc_scr.shape, F32)
    s_a[...] = logits(0)

    def pair(jp, carry):
        kc = 2 * jp
        s_b[...] = logits(kc + 1)
        update(s_a[...], kc)
        s_a[...] = logits(kc + 2)
        update(s_b[...], kc + 1)
        return carry

    lax.fori_loop(0, i // 2, pair, 0)
    key = lax.broadcasted_iota(jnp.int32, (tile, cols), 0)
    qry = lax.broadcasted_iota(jnp.int32, (tile, cols), 1) % tile

    @pl.when(i % 2 == 1)
    def _():
        s_b[...] = logits(i)
        update(s_a[...], i - 1)
        update(jnp.where(key <= qry, s_b[...], NEG), i)

    @pl.when(i % 2 == 0)
    def _():
        update(jnp.where(key <= qry, s_a[...], NEG), i)

    _store_heads(o_ref, acc_scr[...] * (_gate_row(gate_ref) / l_scr[...]), tile)


def _nsa_slc(qt, sbt, ske, svt, gates_t, tq):
    b, g, _, _, t = qt.shape
    gw = NSA_GROUP * HEAD_DIM
    nk, tk = svt.shape[2], svt.shape[4]
    assert tq == tk
    cols = NSA_GROUP * tq
    return pl.pallas_call(
        functools.partial(_nsa_slc_kernel, tile=tq),
        out_shape=jax.ShapeDtypeStruct((b, t, NSA_W), F32),
        grid=(b, g, t // tq),
        in_specs=[
            pl.BlockSpec((None, None, NSA_GROUP, HEAD_DIM, tq), lambda bi, gi, i: (bi, gi, 0, 0, i)),
            pl.BlockSpec((None, None, LANES, tq), lambda bi, gi, i: (bi, gi, 0, i)),
            pl.BlockSpec((None, None, t, 2 * HEAD_DIM), lambda bi, gi, i: (bi, gi, 0, 0)),
            pl.BlockSpec((None, None, nk, HEAD_DIM, tk), lambda bi, gi, i: (bi, gi, 0, 0, 0)),
            pl.BlockSpec((None, None, None, NSA_GROUP, tq), lambda bi, gi, i: (bi, 1, gi, 0, i)),
        ],
        out_specs=pl.BlockSpec((None, tq, gw), lambda bi, gi, i: (bi, i, gi)),
        scratch_shapes=[pltpu.VMEM((tk, cols), F32), pltpu.VMEM((tk, cols), F32),
                        pltpu.VMEM((1, cols), F32), pltpu.VMEM((1, cols), F32), pltpu.VMEM((HEAD_DIM, cols), F32)],
        compiler_params=_params(("parallel", "parallel", "arbitrary")),
        name="nsa_selected",
    )(qt, sbt, ske, svt, gates_t)


def _nsa_win_kernel(qt_ref, k_ref, vt_ref, gate_ref, ocmp_ref, oslc_ref, o_ref, *, tile):
    i = pl.program_id(2)
    q_t = _heads_on_lanes(qt_ref)
    cols = NSA_GROUP * tile
    back = WINDOW // tile
    chunks = [jnp.maximum(i - back + j, 0) for j in range(back + 1)]
    v_t = jnp.concatenate([vt_ref[c] for c in chunks], axis=1)
    r = lax.broadcasted_iota(jnp.int32, (tile, cols), 0)
    c = lax.broadcasted_iota(jnp.int32, (tile, cols), 1) % tile
    blocks = []
    for j in range(back + 1):
        sj = _dot(k_ref[pl.ds(pl.multiple_of(chunks[j] * tile, tile), tile), :], q_t)
        if j == 0:
            sj = jnp.where(r > c, sj, NEG)
        if j == back:
            sj = jnp.where(c >= r, sj, NEG)
        else:
            sj = sj + jnp.where(i - back + j >= 0, 0.0, NEG)
        blocks.append(sj)
    s = jnp.concatenate(blocks, axis=0)
    p = jnp.exp2(s - jnp.max(s, axis=0, keepdims=True))
    l = jnp.sum(p, axis=0, keepdims=True)
    o_t = _dot(v_t, p.astype(BF16)) * (_gate_row(gate_ref) / l)
    for j in range(NSA_GROUP):
        sl = slice(j * HEAD_DIM, (j + 1) * HEAD_DIM)
        o_ref[:, sl] = (ocmp_ref[:, sl] + oslc_ref[:, sl] + o_t[:, j * tile:(j + 1) * tile].T).astype(o_ref.dtype)


def _nsa_win(qt, wkr, wvt, gates_t, o_cmp, o_slc, tile):
    b, g, _, _, t = qt.shape
    gw = NSA_GROUP * HEAD_DIM
    nk = t // tile
    assert WINDOW % tile == 0 and WINDOW >= tile
    rows = pl.BlockSpec((None, tile, gw), lambda bi, gi, i: (bi, i, gi))
    return pl.pallas_call(
        functools.partial(_nsa_win_kernel, tile=tile),
        out_shape=jax.ShapeDtypeStruct((b, t, NSA_W), BF16),
        grid=(b, g, nk),
        in_specs=[
            pl.BlockSpec((None, None, NSA_GROUP, HEAD_DIM, tile), lambda bi, gi, i: (bi, gi, 0, 0, i)),
            pl.BlockSpec((None, None, t, HEAD_DIM), lambda bi, gi, i: (bi, gi, 0, 0)),
            pl.BlockSpec((None, None, nk, HEAD_DIM, tile), lambda bi, gi, i: (bi, gi, 0, 0, 0)),
            pl.BlockSpec((None, None, None, NSA_GROUP, tile), lambda bi, gi, i: (bi, 2, gi, 0, i)),
            rows, rows,
        ],
        out_specs=rows,
        compiler_params=_params(("parallel", "parallel", "parallel")),
        name="nsa_window",
    )(qt, wkr, wvt, gates_t, o_cmp, o_slc)


def _mem_attn_kernel(q_ref, k_ref, v_ref, o_ref, *, scale):
    s = _dot_nt(q_ref[...], k_ref[...]) * scale
    e = jnp.exp(s - jnp.max(s, axis=-1, keepdims=True))
    p = e / jnp.sum(e, axis=-1, keepdims=True)
    o_ref[...] = _dot(p.astype(BF16), v_ref[...]).astype(o_ref.dtype)


def _mem_attention(proj, mkv, off_mq, tq):
    b, t, _ = proj.shape
    m = mkv.shape[1]
    d = MEM_HEAD_DIM
    return pl.pallas_call(
        functools.partial(_mem_attn_kernel, scale=d ** -0.5),
        out_shape=jax.ShapeDtypeStruct((b, t, MEM_W), BF16),
        grid=(b, MEM_HEADS, t // tq),
        in_specs=[
            pl.BlockSpec((None, tq, d), lambda bi, h, i: (bi, i, off_mq // d + h)),
            pl.BlockSpec((None, m, d), lambda bi, h, i: (bi, 0, h)),
            pl.BlockSpec((None, m, d), lambda bi, h, i: (bi, 0, MEM_HEADS + h)),
        ],
        out_specs=pl.BlockSpec((None, tq, d), lambda bi, h, i: (bi, i, h)),
        compiler_params=_params(("parallel", "parallel", "parallel")),
        name="mem_attention",
    )(proj, mkv, mkv)


def _merge_kernel(ofox_ref, onsa_ref, omem_ref, bg0_ref, bg1_ref, bg2_ref, wf_ref, wn_ref, wm_ref, o_ref):
    sig = lambda r: _sigmoid(r[...].astype(F32))
    o_ref[...] = (sig(bg0_ref) * _dot(ofox_ref[...], wf_ref[...])
                  + sig(bg1_ref) * _dot(onsa_ref[...], wn_ref[...])
                  + sig(bg2_ref) * _dot(omem_ref[...], wm_ref[...])).astype(o_ref.dtype)


def _merge(ofox, onsa, omem, proj, wf, wn, wm, tm, tn):
    m, dm = ofox.shape[0], wf.shape[1]
    row = lambda w: pl.BlockSpec((tm, w), lambda j, i: (i, 0))
    gate = lambda br: pl.BlockSpec((tm, tn), lambda j, i: (i, (OFF_BG + br * dm) // tn + j))
    wspec = lambda w: pl.BlockSpec((w.shape[0], tn), lambda j, i: (0, j))
    return pl.pallas_call(
        _merge_kernel,
        out_shape=jax.ShapeDtypeStruct((m, dm), BF16),
        grid=(dm // tn, m // tm),
        in_specs=[row(FOX_W), row(NSA_W), row(MEM_W), gate(0), gate(1), gate(2), wspec(wf), wspec(wn), wspec(wm)],
        out_specs=pl.BlockSpec((tm, tn), lambda j, i: (i, j)),
        compiler_params=_params(("parallel", "parallel")),
        name="branch_merge",
    )(ofox, onsa, omem, proj, proj, proj, wf, wn, wm)


def _out_proj_kernel(m_ref, w_ref, g_ref, h_ref, o_ref):
    o_ref[...] = h_ref[...] + _rms(_dot(m_ref[...], w_ref[...]), g_ref[...])


def _out_proj(merged, w, g, h, tm):
    m, d = h.shape
    return pl.pallas_call(
        _out_proj_kernel,
        out_shape=jax.ShapeDtypeStruct((m, d), F32),
        grid=(m // tm,),
        in_specs=[
            pl.BlockSpec((tm, d), lambda i: (i, 0)),
            pl.BlockSpec((d, d), lambda i: (0, 0)),
            pl.BlockSpec((1, d), lambda i: (0, 0)),
            pl.BlockSpec((tm, d), lambda i: (i, 0)),
        ],
        out_specs=pl.BlockSpec((tm, d), lambda i: (i, 0)),
        compiler_params=_params(("parallel",)),
        name="out_proj",
    )(merged, w, g.reshape(1, d), h)


def _mlp_kernel(h_ref, gpre_ref, w1_ref, w2_ref, gpost_ref, o_ref, u_ref, acc_ref):
    f = pl.program_id(1)

    @pl.when(f == 0)
    def _():
        u_ref[...] = _rms(h_ref[...], gpre_ref[...]).astype(BF16)
        acc_ref[...] = jnp.zeros_like(acc_ref)

    a = jnp.square(jnp.maximum(_dot(u_ref[...], w1_ref[...]), 0.0))
    acc_ref[...] += _dot(a.astype(BF16), w2_ref[...])

    @pl.when(f == pl.num_programs(1) - 1)
    def _():
        o_ref[...] = h_ref[...] + _rms(acc_ref[...], gpost_ref[...])


def _mlp(h, gpre, w1, w2, gpost, tm, tf):
    m, d = h.shape
    dff = w1.shape[1]
    return pl.pallas_call(
        _mlp_kernel,
        out_shape=jax.ShapeDtypeStruct((m, d), F32),
        grid=(m // tm, dff // tf),
        in_specs=[
            pl.BlockSpec((tm, d), lambda i, f: (i, 0)),
            pl.BlockSpec((1, d), lambda i, f: (0, 0)),
            pl.BlockSpec((d, tf), lambda i, f: (0, f)),
            pl.BlockSpec((tf, d), lambda i, f: (f, 0)),
            pl.BlockSpec((1, d), lambda i, f: (0, 0)),
        ],
        out_specs=pl.BlockSpec((tm, d), lambda i, f: (i, 0)),
        scratch_shapes=[pltpu.VMEM((tm, d), BF16), pltpu.VMEM((tm, d), F32)],
        compiler_params=_params(("parallel", "arbitrary")),
        name="mlp",
    )(h, gpre.reshape(1, d), w1, w2, gpost.reshape(1, d))


def _rope_tables(t):
    half = ROPE_DIM // 2
    inv_freq = ROPE_THETA ** (-jnp.arange(half, dtype=F32) / half)
    ang = jnp.arange(t, dtype=jnp.int32).astype(F32)[:, None] * inv_freq[None, :]
    cos, sin = jnp.cos(ang), jnp.sin(ang)
    pad = HEAD_DIM - ROPE_DIM
    cos_t = jnp.concatenate([cos, cos, jnp.ones((t, pad), F32)], axis=1)
    sin_t = jnp.concatenate([-sin, sin, jnp.zeros((t, pad), F32)], axis=1)
    return cos_t, sin_t


def _w_in_kernel(x_ref, o_ref):
    o_ref[...] = x_ref[0].T.astype(o_ref.dtype)


def _rows_kernel(x_ref, o_ref):
    o_ref[...] = x_ref[0]


def _split_w_in(w_in_t, layer, d_model, tc):
    widths = (FOX_W, FOX_W, FOX_W, FOX_HEADS, NSA_W) + (NSA_KV_W,) * 6 + (3 * NSA_HEADS, MEM_W, N_BRANCHES * d_model)
    offs = [int(o) for o in np.concatenate([[0], np.cumsum(widths)])]
    runs = ((0, 3), (4, 5), (13, 14), (5, 11), (12, 13))
    n_main = sum(offs[last] - offs[first] for first, last in runs)
    _, _, d = w_in_t.shape
    bounds, shifts, dst = [], [], 0
    for first, last in runs:
        src, width = offs[first], offs[last] - offs[first]
        assert width % tc == 0 and dst % tc == 0 and (src - dst) % SUBLANES == 0
        dst += width
        bounds.append(dst // tc)
        shifts.append(src - (dst - width))

    def src_rows(j):
        shift = shifts[-1]
        for bound, s in zip(reversed(bounds[:-1]), reversed(shifts[:-1])):
            shift = jnp.where(j < bound, s, shift)
        return layer, pl.multiple_of(j * tc + shift, SUBLANES), 0

    main = pl.pallas_call(
        _w_in_kernel,
        out_shape=jax.ShapeDtypeStruct((d, n_main), BF16),
        grid=(n_main // tc,),
        in_specs=[pl.BlockSpec((pl.Element(1), pl.Element(tc), pl.Element(d)), src_rows)],
        out_specs=pl.BlockSpec((d, tc), lambda j: (0, j)),
        compiler_params=_params(("parallel",)),
        name="w_in_layout",
    )(w_in_t)

    def take_rows(start, count):
        return pl.pallas_call(
            _rows_kernel,
            out_shape=jax.ShapeDtypeStruct((count, d), w_in_t.dtype),
            grid=(1,),
            in_specs=[pl.BlockSpec((pl.Element(1), pl.Element(count), pl.Element(d)), lambda j: (layer, start, 0))],
            out_specs=pl.BlockSpec((count, d), lambda j: (0, 0)),
            name="w_in_rows",
        )(w_in_t)

    narrow = jnp.concatenate([take_rows(offs[3], widths[3]), take_rows(offs[11], widths[11]),
                              jnp.zeros((SMALL_W - widths[3] - widths[11], d), w_in_t.dtype)], axis=0)
    return main, narrow.T.astype(BF16)


def _pick(n, candidates):
    for c in candidates:
        if n % c == 0:
            return c
    raise ValueError(f"no tile for {n}")


def kernel(x, mem, w_in, b_f, w_cmp1_k, w_cmp2_k, pe_cmp_k, w_cmp1_v, w_cmp2_v, pe_cmp_v, w_mem_kv, g_mem, w_up_fox, w_up_nsa, w_up_mem, w_o, g_pre_mix, g_post_mix, g_pre_mlp, g_post_mlp, w_mlp1, w_mlp2):
    b, t, d = x.shape
    depth = w_in.shape[0]
    n = b * t
    mlen = mem.shape[1]
    off_kv = OFF_BG + N_BRANCHES * d
    off_mq = off_kv + 6 * NSA_KV_W
    n_main = off_mq + MEM_W

    cos_t, sin_t = _rope_tables(t)
    tm_proj = _pick(n, (1024, 512, 256))
    tn_proj = _pick(n_main, (1280, 1024, 512, 256, 128))
    attn_tile = _pick(t, (512, 256, 128))
    fox_tq = _pick(t, (2048, 1024, 512, 256, 128))
    nsa_tile = _pick(t, (256, 128))
    tm_mem = _pick(b * mlen, (1024, 512, 256))

    h = x.reshape(n, d)
    mem2 = mem.reshape(b * mlen, d)
    w_in_t = jnp.swapaxes(w_in, 1, 2)
    for l in range(depth):
        w_main, w_small = _split_w_in(w_in_t, l, d, 512)
        proj, small = _norm_matmul(h, g_pre_mix[l], w_main, w_small, tm_proj, tn_proj, "in_proj")
        proj = proj.reshape(b, t, n_main)

        ff = small[:, SMALL_FF:SMALL_FF + FOX_HEADS].reshape(b, t, FOX_HEADS).transpose(0, 2, 1)
        c = _fox_cumsum(ff, b_f[l])
        o_fox = _fox_attention(proj, c, fox_tq, attn_tile)

        kc, vc, ske, wkr, svt, wvt = _nsa_prep(
            proj, cos_t, sin_t,
            w_cmp1_k[l].astype(BF16), w_cmp2_k[l].astype(BF16), pe_cmp_k[l],
            w_cmp1_v[l].astype(BF16), w_cmp2_v[l].astype(BF16), pe_cmp_v[l], off_kv, 2 * nsa_tile, nsa_tile)
        ng = small[:, SMALL_NG:SMALL_NG + 3 * NSA_HEADS].reshape(b, t, 3, NSA_KV_HEADS, NSA_GROUP)
        gates_cmp = ng[:, :, 0].transpose(0, 2, 1, 3)
        gates_t = ng.transpose(0, 2, 3, 4, 1)
        qt, o_cmp, sbt = _nsa_cmp(proj, cos_t, sin_t, kc, vc, gates_cmp, nsa_tile)
        o_slc = _nsa_slc(qt, sbt, ske, svt, gates_t, 2 * nsa_tile)
        o_nsa = _nsa_win(qt, wkr, wvt, gates_t, o_cmp, o_slc, nsa_tile)

        half = lambda w: _layer_bf16(w, l, w.shape[1] // 2)
        (mkv,) = _norm_matmul(mem2, g_mem[l], half(w_mem_kv), None, tm_mem, 1024, "mem_kv")
        o_mem = _mem_attention(proj, mkv.reshape(b, mlen, 2 * MEM_W), off_mq, 2 * attn_tile)

        merged = _merge(o_fox.reshape(n, FOX_W), o_nsa.reshape(n, NSA_W), o_mem.reshape(n, MEM_W), proj.reshape(n, n_main),
                        half(w_up_fox), half(w_up_nsa), half(w_up_mem), 512, 1024)
        h = _out_proj(merged, half(w_o), g_post_mix[l], h, 512)
        w1 = _layer_bf16(w_mlp1, l, w_mlp1.shape[1] // 8)
        w2 = _layer_bf16(w_mlp2, l, w_mlp2.shape[1] // 8)
        h = _mlp(h, g_pre_mlp[l], w1, w2, g_post_mlp[l], 512, 1024)
    return h.reshape(b, t, d)
```

```python
import functools
from typing import NamedTuple

import numpy as np
import jax
import jax.numpy as jnp
from jax import lax
from jax.experimental import pallas as pl
from jax.experimental.pallas import tpu as pltpu

F32 = jnp.float32
BF16 = jnp.bfloat16

HEAD_DIM = 128
FOX_HEADS = 8
NSA_HEADS = 8
NSA_KV_HEADS = 2
NSA_GROUP = NSA_HEADS // NSA_KV_HEADS
MEM_HEADS = 4
MEM_HEAD_DIM = 256
N_BRANCHES = 3
CMP_LEN = 32
CMP_STRIDE = 16
SLC_BLOCK = 64
SLC_TOPK = 16
WINDOW = 512
ROPE_THETA = 500000.0
ROPE_DIM = HEAD_DIM // 4
RMS_EPS = 1e-6
NEG = -1e30
FORCED_SCORE = 1e6
LOG2E = 1.4426950408889634

LANES = 128
SUBLANES = 8
VMEM_LIMIT = 56 * 1024 * 1024

FOX_W = FOX_HEADS * HEAD_DIM
NSA_W = NSA_HEADS * HEAD_DIM
NSA_KV_W = NSA_KV_HEADS * HEAD_DIM
MEM_W = MEM_HEADS * MEM_HEAD_DIM
OFF_FQ = 0
OFF_FK = OFF_FQ + FOX_W
OFF_FV = OFF_FK + FOX_W
OFF_NQ = OFF_FV + FOX_W
OFF_BG = OFF_NQ + NSA_W
SMALL_W = LANES
SMALL_FF = 0
SMALL_NG = FOX_HEADS


def _params(sem):
    return pltpu.CompilerParams(dimension_semantics=sem, vmem_limit_bytes=VMEM_LIMIT)


def _rms(x, g):
    return (x * lax.rsqrt(jnp.mean(x * x, axis=-1, keepdims=True) + RMS_EPS)) * g


def _sigmoid(x):
    return 0.5 * jnp.tanh(0.5 * x) + 0.5


def _dot(a, b):
    return jnp.dot(a, b, preferred_element_type=F32)


def _dot_nt(a, b):
    return lax.dot_general(a, b, (((1,), (1,)), ((), ())), preferred_element_type=F32)


def _cast_kernel(x_ref, o_ref):
    o_ref[...] = x_ref[...].astype(o_ref.dtype)


def _layer_bf16(w, layer, tr):
    _, r, c = w.shape
    return pl.pallas_call(
        _cast_kernel,
        out_shape=jax.ShapeDtypeStruct((r, c), BF16),
        grid=(r // tr,),
        in_specs=[pl.BlockSpec((None, tr, c), lambda i: (layer, i, 0))],
        out_specs=pl.BlockSpec((tr, c), lambda i: (i, 0)),
        compiler_params=_params(("parallel",)),
        name="weight_bf16",
    )(w)


def _norm_matmul_kernel(x_ref, g_ref, w_ref, *rest, narrow):
    if narrow:
        ws_ref, o_ref, os_ref, u_ref = rest
    else:
        o_ref, u_ref = rest

    @pl.when(pl.program_id(1) == 0)
    def _():
        u_ref[...] = _rms(x_ref[...], g_ref[...]).astype(BF16)
        if narrow:
            os_ref[...] = _dot(u_ref[...], ws_ref[...])

    o_ref[...] = _dot(u_ref[...], w_ref[...]).astype(o_ref.dtype)


def _norm_matmul(x, g, w, ws, tm, tn, name):
    m, d = x.shape
    n = w.shape[1]
    narrow = ws is not None
    in_specs = [
        pl.BlockSpec((tm, d), lambda i, j: (i, 0)),
        pl.BlockSpec((1, d), lambda i, j: (0, 0)),
        pl.BlockSpec((d, tn), lambda i, j: (0, j)),
    ]
    out_shape = [jax.ShapeDtypeStruct((m, n), BF16)]
    out_specs = [pl.BlockSpec((tm, tn), lambda i, j: (i, j))]
    args = [x, g.reshape(1, d), w]
    if narrow:
        in_specs.append(pl.BlockSpec((d, SMALL_W), lambda i, j: (0, 0)))
        out_shape.append(jax.ShapeDtypeStruct((m, SMALL_W), F32))
        out_specs.append(pl.BlockSpec((tm, SMALL_W), lambda i, j: (i, 0)))
        args.append(ws)
    return pl.pallas_call(
        functools.partial(_norm_matmul_kernel, narrow=narrow),
        out_shape=tuple(out_shape),
        grid=(m // tm, n // tn),
        in_specs=in_specs,
        out_specs=tuple(out_specs),
        scratch_shapes=[pltpu.VMEM((tm, d), BF16)],
        compiler_params=_params(("parallel", "arbitrary")),
        name=name,
    )(*args)


def _fox_cumsum_kernel(bf_ref, ff_ref, c_ref):
    heads, rows, lanes = ff_ref.shape
    lane = lax.broadcasted_iota(jnp.int32, (rows, lanes), 1)
    row = lax.broadcasted_iota(jnp.int32, (rows, lanes), 0)
    for hh in range(heads):
        x = ff_ref[hh] + bf_ref[hh]
        y = jnp.minimum(x, 0.0) - jnp.log1p(jnp.exp(-jnp.abs(x)))
        s = 1
        while s < lanes:
            y = y + jnp.where(lane >= s, pltpu.roll(y, s, axis=1), 0.0)
            s *= 2
        tot = jnp.broadcast_to(y[:, lanes - 1:lanes], y.shape)
        z = tot
        s = 1
        while s < rows:
            z = z + jnp.where(row >= s, pltpu.roll(z, s, axis=0), 0.0)
            s *= 2
        c_ref[hh] = (y + (z - tot)) * LOG2E


def _fox_cumsum(ff, b_f):
    b, h, t = ff.shape
    rows = t // LANES
    out = pl.pallas_call(
        _fox_cumsum_kernel,
        out_shape=jax.ShapeDtypeStruct((b, h, rows, LANES), F32),
        grid=(b,),
        in_specs=[
            pl.BlockSpec(memory_space=pltpu.SMEM),
            pl.BlockSpec((None, h, rows, LANES), lambda i: (i, 0, 0, 0)),
        ],
        out_specs=pl.BlockSpec((None, h, rows, LANES), lambda i: (i, 0, 0, 0)),
        compiler_params=_params(("parallel",)),
        name="fox_cumsum",
    )(b_f, ff.reshape(b, h, rows, LANES))
    return out.reshape(b, h, t)


def _flash_t_step(carry, s, v_t):
    m, l, acc = carry
    m_new = jnp.maximum(m, jnp.max(s, axis=0, keepdims=True))
    alpha = jnp.exp2(m - m_new)
    p = jnp.exp2(s - m_new)
    l = alpha * l + jnp.sum(p, axis=0, keepdims=True)
    acc = alpha * acc + _dot(v_t, p.astype(BF16))
    return m_new, l, acc


FOX_C_TERMS = 3


def _fox_attn_kernel(q_ref, k_ref, v_ref, c_ref, o_ref, ke_scr, vt_scr, s_a, s_b, m_scr, l_scr, acc_scr,
                     *, tq, tk, n_tiles, qscale):
    i = pl.program_id(2)
    nk = vt_scr.shape[0]
    per_tile = tq // tk

    @pl.when(i == 0)
    def _():
        row = lax.broadcasted_iota(jnp.int32, (HEAD_DIM, tk), 0)
        for kc in range(nk):
            rows = slice(kc * tk, (kc + 1) * tk)
            vt_scr[kc] = v_ref[rows, :].astype(F32).T.astype(BF16)
            ke_scr[rows, 0:HEAD_DIM] = k_ref[rows, :]
            c = c_ref[:, rows]
            terms = jnp.zeros((HEAD_DIM, tk), F32)
            for n in range(FOX_C_TERMS):
                part = c.astype(BF16).astype(F32)
                terms = jnp.where(row == n, part, terms)
                c = c - part
            ke_scr[rows, HEAD_DIM:2 * HEAD_DIM] = terms.T.astype(BF16)

    minus_ones = jnp.where(lax.broadcasted_iota(jnp.int32, (HEAD_DIM, tq), 0) < FOX_C_TERMS, -1.0, 0.0).astype(BF16)
    q_t = jnp.concatenate([(q_ref[...].astype(F32) * qscale).T.astype(BF16), minus_ones], axis=0)

    def logits(kc, lo):
        return _dot(ke_scr[kc * tk:(kc + 1) * tk, :], q_t[:, lo:])

    def run(tile):
        chunks = [(kc, 0, False) for kc in range(tile * per_tile)]
        chunks += [(tile * per_tile + d, d * tk, True) for d in range(per_tile)]
        m_scr[...] = jnp.full(m_scr.shape, -jnp.inf, F32)
        l_scr[...] = jnp.zeros(l_scr.shape, F32)
        acc_scr[...] = jnp.zeros(acc_scr.shape, F32)
        bufs = (s_a, s_b)
        bufs[0][:, chunks[0][1]:] = logits(chunks[0][0], chunks[0][1])
        for idx, (kc, lo, masked) in enumerate(chunks):
            if idx + 1 < len(chunks):
                kn, ln, _ = chunks[idx + 1]
                bufs[(idx + 1) % 2][:, ln:] = logits(kn, ln)
            s = bufs[idx % 2][:, lo:]
            if masked:
                key = lax.broadcasted_iota(jnp.int32, (tk, tq - lo), 0)
                col = lax.broadcasted_iota(jnp.int32, (tk, tq - lo), 1)
                s = jnp.where(key <= col, s, NEG)
            m, l, acc = _flash_t_step((m_scr[:, lo:], l_scr[:, lo:], acc_scr[:, lo:]), s, vt_scr[kc])
            m_scr[:, lo:] = m
            l_scr[:, lo:] = l
            acc_scr[:, lo:] = acc
        o_ref[...] = (acc_scr[...] / l_scr[...]).T.astype(o_ref.dtype)

    for tile in range(n_tiles):
        pl.when(i == tile)(functools.partial(run, tile))


def _fox_attention(proj, c2, tq, tk):
    b, t, _ = proj.shape
    nk = t // tk
    n_tiles = t // tq
    assert tq % tk == 0
    qb, kb, vb = OFF_FQ // HEAD_DIM, OFF_FK // HEAD_DIM, OFF_FV // HEAD_DIM
    return pl.pallas_call(
        functools.partial(_fox_attn_kernel, tq=tq, tk=tk, n_tiles=n_tiles, qscale=HEAD_DIM ** -0.5 * LOG2E),
        out_shape=jax.ShapeDtypeStruct((b, t, FOX_W), BF16),
        grid=(b, FOX_HEADS, n_tiles),
        in_specs=[
            pl.BlockSpec((None, tq, HEAD_DIM), lambda bi, h, i: (bi, i, qb + h)),
            pl.BlockSpec((None, t, HEAD_DIM), lambda bi, h, i: (bi, 0, kb + h)),
            pl.BlockSpec((None, t, HEAD_DIM), lambda bi, h, i: (bi, 0, vb + h)),
            pl.BlockSpec((None, None, 1, t), lambda bi, h, i: (bi, h, 0, 0)),
        ],
        out_specs=pl.BlockSpec((None, tq, HEAD_DIM), lambda bi, h, i: (bi, i, h)),
        scratch_shapes=[
            pltpu.VMEM((t, 2 * HEAD_DIM), BF16), pltpu.VMEM((nk, HEAD_DIM, tk), BF16),
            pltpu.VMEM((tk, tq), F32), pltpu.VMEM((tk, tq), F32),
            pltpu.VMEM((1, tq), F32), pltpu.VMEM((1, tq), F32), pltpu.VMEM((HEAD_DIM, tq), F32),
        ],
        compiler_params=_params(("parallel", "parallel", "arbitrary")),
        name="fox_attention",
    )(proj, proj, proj, c2.reshape(b, FOX_HEADS, 1, t))


def _rope(x, cos, sin):
    half = ROPE_DIM // 2
    lane = lax.broadcasted_iota(jnp.int32, x.shape, 1)
    swapped = jnp.where(lane < half, pltpu.roll(x, LANES - half, axis=1), pltpu.roll(x, half, axis=1))
    return x * cos + swapped * sin


def _compress(xs_ref, x, w1_ref, w2_ref, pe_ref):
    t = x.shape[0]
    nch = t // CMP_STRIDE
    xs_ref[...] = x
    pe = pe_ref[...]
    first, second = [], []
    for p in range(CMP_STRIDE):
        xp = xs_ref[pl.ds(p, nch, stride=CMP_STRIDE), :]
        first.append((xp + pe[p:p + 1]).astype(BF16))
        second.append((xp + pe[CMP_STRIDE + p:CMP_STRIDE + p + 1]).astype(BF16))
    kw = CMP_STRIDE * HEAD_DIM
    a = _dot(jnp.concatenate(first, axis=1), w1_ref[0:kw, :])
    b = _dot(jnp.concatenate(second, axis=1), w1_ref[kw:2 * kw, :])
    hidden = jax.nn.gelu(a + pltpu.roll(b, nch - 1, axis=0))
    return _dot(hidden.astype(BF16), w2_ref[...])


def _store_transposed_chunks(dst_ref, src_ref):
    nk, _, tile = dst_ref.shape
    for c in range(nk):
        dst_ref[c] = src_ref[c * tile:(c + 1) * tile, :].astype(F32).T.astype(dst_ref.dtype)


def _nsa_prep_kernel(ck_ref, cv_ref, sk_ref, sv_ref, wk_ref, wv_ref, cos_ref, sin_ref,
                     w1k_ref, w2k_ref, pek_ref, w1v_ref, w2v_ref, pev_ref,
                     kc_ref, vc_ref, ske_ref, wkr_ref, svt_ref, wvt_ref, xs_ref):
    cos = cos_ref[...]
    sin = sin_ref[...]
    t = cos.shape[0]
    _store_transposed_chunks(svt_ref, sv_ref)
    _store_transposed_chunks(wvt_ref, wv_ref)
    kc_ref[...] = _compress(xs_ref, _rope(ck_ref[...].astype(F32), cos, sin), w1k_ref, w2k_ref, pek_ref).astype(BF16)
    vc_ref[...] = _compress(xs_ref, cv_ref[...].astype(F32), w1v_ref, w2v_ref, pev_ref).astype(BF16)
    ske_ref[:, 0:HEAD_DIM] = _rope(sk_ref[...].astype(F32), cos, sin).astype(BF16)
    row = lax.broadcasted_iota(jnp.int32, (t, LANES), 0)
    lane = lax.broadcasted_iota(jnp.int32, (t, LANES), 1)
    ske_ref[:, HEAD_DIM:2 * HEAD_DIM] = jnp.where(row // SLC_BLOCK == lane, 1.0, 0.0).astype(BF16)
    wkr_ref[...] = _rope(wk_ref[...].astype(F32), cos, sin).astype(BF16)


def _nsa_prep(proj, cos, sin, w1k, w2k, pek, w1v, w2v, pev, off_kv, slc_tk, win_tk):
    b, t, _ = proj.shape
    g = NSA_KV_HEADS
    nch = t // CMP_STRIDE
    blk = lambda sec: (lambda bi, gi: (bi, 0, (off_kv + sec * NSA_KV_W) // HEAD_DIM + gi))
    full = lambda shape: pl.BlockSpec(shape, lambda bi, gi: (0,) * len(shape))
    per_head = lambda shape: pl.BlockSpec((None, None) + shape, lambda bi, gi: (bi, gi) + (0,) * len(shape))
    return pl.pallas_call(
        _nsa_prep_kernel,
        out_shape=(
            jax.ShapeDtypeStruct((b, g, nch, HEAD_DIM), BF16),
            jax.ShapeDtypeStruct((b, g, nch, HEAD_DIM), BF16),
            jax.ShapeDtypeStruct((b, g, t, 2 * HEAD_DIM), BF16),
            jax.ShapeDtypeStruct((b, g, t, HEAD_DIM), BF16),
            jax.ShapeDtypeStruct((b, g, t // slc_tk, HEAD_DIM, slc_tk), BF16),
            jax.ShapeDtypeStruct((b, g, t // win_tk, HEAD_DIM, win_tk), BF16),
        ),
        grid=(b, g),
        in_specs=[pl.BlockSpec((None, t, HEAD_DIM), blk(sec)) for sec in range(6)] + [
            full((t, LANES)), full((t, LANES)),
            full(w1k.shape), full(w2k.shape), full(pek.shape),
            full(w1v.shape), full(w2v.shape), full(pev.shape),
        ],
        out_specs=(
            per_head((nch, HEAD_DIM)), per_head((nch, HEAD_DIM)),
            per_head((t, 2 * HEAD_DIM)), per_head((t, HEAD_DIM)),
            per_head((t // slc_tk, HEAD_DIM, slc_tk)), per_head((t // win_tk, HEAD_DIM, win_tk)),
        ),
        scratch_shapes=[pltpu.VMEM((t, HEAD_DIM), F32)],
        compiler_params=_params(("parallel", "parallel")),
        name="nsa_prep",
    )(proj, proj, proj, proj, proj, proj, cos, sin, w1k, w2k, pek, w1v, w2v, pev)


def _nsa_cmp_kernel(q_ref, cos_ref, sin_ref, kc_ref, vc_ref, gate_ref, qt_ref, ocmp_ref, sbt_ref, *, tq, scale):
    q0 = pl.program_id(2) * tq
    gates = _sigmoid(gate_ref[...])
    cos = cos_ref[...]
    sin = sin_ref[...]
    kc = kc_ref[...]
    vc = vc_ref[...]
    nch = kc.shape[0]
    n_real = nch - 1
    nsel = (nch * CMP_STRIDE) // SLC_BLOCK
    tpos = q0 + lax.broadcasted_iota(jnp.int32, (tq, nch), 0)
    nidx = lax.broadcasted_iota(jnp.int32, (tq, nch), 1)
    valid = (nidx * CMP_STRIDE + (CMP_LEN - 1) <= tpos) & (nidx < n_real)

    psum = jnp.zeros((tq, nch), F32)
    for j in range(NSA_GROUP):
        sl = slice(j * HEAD_DIM, (j + 1) * HEAD_DIM)
        qf = _rope(q_ref[:, sl].astype(F32), cos, sin)
        qt_ref[j] = (qf * (scale * LOG2E)).T.astype(BF16)
        qj = qf.astype(BF16)
        lg = jnp.where(valid, _dot_nt(qj, kc) * scale, NEG)
        e = jnp.where(valid, jnp.exp(lg - jnp.max(lg, axis=-1, keepdims=True)), 0.0)
        den = jnp.sum(e, axis=-1, keepdims=True)
        inv = 1.0 / jnp.where(den > 0.0, den, 1.0)
        ocmp_ref[:, sl] = _dot((e * (inv * gates[:, j:j + 1])).astype(BF16), vc).astype(ocmp_ref.dtype)
        psum = psum + e * inv

    ratio = SLC_BLOCK // CMP_STRIDE
    jrow = lax.broadcasted_iota(jnp.int32, (nsel, nch), 0)
    ncol = lax.broadcasted_iota(jnp.int32, (nsel, nch), 1)
    lo_n = jrow * ratio - (CMP_LEN - 1) // CMP_STRIDE
    hi_n = jrow * ratio + (SLC_BLOCK - 1) // CMP_STRIDE
    overlap_t = jnp.where((ncol >= lo_n) & (ncol <= hi_n) & (ncol < n_real), 1.0, 0.0).astype(BF16)
    p_hi = psum.astype(BF16)
    r1 = psum - p_hi.astype(F32)
    p_mid = r1.astype(BF16)
    p_lo = (r1 - p_mid.astype(F32)).astype(BF16)
    pslc_t = _dot_nt(overlap_t, p_hi) + _dot_nt(overlap_t, p_mid) + _dot_nt(overlap_t, p_lo)

    blk = lax.broadcasted_iota(jnp.int32, (nsel, tq), 0)
    cur = (q0 + lax.broadcasted_iota(jnp.int32, (nsel, tq), 1)) // SLC_BLOCK
    forced = (blk == 0) | (blk == cur) | (blk == cur - 1)
    score = jnp.where(forced, FORCED_SCORE, jnp.where(blk <= cur, pslc_t, -1.0))

    sub = SUBLANES
    n_groups = nsel // sub
    groups = [score[g * sub:(g + 1) * sub, :] for g in range(n_groups)]
    ranks = [jnp.zeros((sub, tq), F32) for _ in range(n_groups)]
    row_in_group = lax.broadcasted_iota(jnp.int32, (sub, tq), 0)
    for i in range(nsel):
        gi, ri = divmod(i, sub)
        si = score[i:i + 1, :]
        for g in range(n_groups):
            if g > gi:
                beats = si >= groups[g]
            elif g < gi:
                beats = si > groups[g]
            else:
                beats = (si > groups[g]) | ((si == groups[g]) & (row_in_group > ri))
            ranks[g] = ranks[g] + jnp.where(beats, 1.0, 0.0)
    rank = jnp.concatenate(ranks, axis=0)
    k_top = min(SLC_TOPK, nsel)
    bias_t = jnp.where(rank < k_top, 0.0, NEG)
    sbt_ref[...] = jnp.concatenate([bias_t, jnp.zeros((LANES - nsel, tq), F32)], axis=0).astype(BF16)


def _nsa_cmp(proj, cos, sin, kc, vc, gates_cmp, tq):
    b, t, _ = proj.shape
    g = NSA_KV_HEADS
    gw = NSA_GROUP * HEAD_DIM
    nch = kc.shape[2]
    assert (nch * CMP_STRIDE) // SLC_BLOCK <= LANES // 2
    return pl.pallas_call(
        functools.partial(_nsa_cmp_kernel, tq=tq, scale=HEAD_DIM ** -0.5),
        out_shape=(
            jax.ShapeDtypeStruct((b, g, NSA_GROUP, HEAD_DIM, t), BF16),
            jax.ShapeDtypeStruct((b, t, NSA_W), F32),
            jax.ShapeDtypeStruct((b, g, LANES, t), BF16),
        ),
        grid=(b, g, t // tq),
        in_specs=[
            pl.BlockSpec((None, tq, gw), lambda bi, gi, i: (bi, i, OFF_NQ // gw + gi)),
            pl.BlockSpec((tq, LANES), lambda bi, gi, i: (i, 0)),
            pl.BlockSpec((tq, LANES), lambda bi, gi, i: (i, 0)),
            pl.BlockSpec((None, None, nch, HEAD_DIM), lambda bi, gi, i: (bi, gi, 0, 0)),
            pl.BlockSpec((None, None, nch, HEAD_DIM), lambda bi, gi, i: (bi, gi, 0, 0)),
            pl.BlockSpec((None, None, tq, NSA_GROUP), lambda bi, gi, i: (bi, gi, i, 0)),
        ],
        out_specs=(
            pl.BlockSpec((None, None, NSA_GROUP, HEAD_DIM, tq), lambda bi, gi, i: (bi, gi, 0, 0, i)),
            pl.BlockSpec((None, tq, gw), lambda bi, gi, i: (bi, i, gi)),
            pl.BlockSpec((None, None, LANES, tq), lambda bi, gi, i: (bi, gi, 0, i)),
        ),
        compiler_params=_params(("parallel", "parallel", "parallel")),
        name="nsa_cmp_select",
    )(proj, cos, sin, kc, vc, gates_cmp)


def _heads_on_lanes(qt_ref, extra=None):
    parts = []
    for j in range(NSA_GROUP):
        parts.append(qt_ref[j] if extra is None else jnp.concatenate([qt_ref[j], extra], axis=0))
    return jnp.concatenate(parts, axis=1)


def _store_heads(o_ref, o_t, tile):
    for j in range(NSA_GROUP):
        o_ref[:, j * HEAD_DIM:(j + 1) * HEAD_DIM] = o_t[:, j * tile:(j + 1) * tile].T.astype(o_ref.dtype)


def _gate_row(gate_ref):
    g = _sigmoid(gate_ref[...])
    return jnp.concatenate([g[j:j + 1, :] for j in range(NSA_GROUP)], axis=1)


def _nsa_slc_kernel(qt_ref, sbt_ref, ke_ref, vt_ref, gate_ref, o_ref, s_a, s_b, m_scr, l_scr, acc_scr, *, tile):
    i = pl.program_id(2)
    q_t = _heads_on_lanes(qt_ref, sbt_ref[...])
    cols = NSA_GROUP * tile

    def logits(kc):
        k = ke_ref[pl.ds(pl.multiple_of(kc * tile, tile), tile), :]
        return _dot(k, q_t)

    def update(s, kc):
        m, l, acc = _flash_t_step((m_scr[...], l_scr[...], acc_scr[...]), s, vt_ref[kc])
        m_scr[...] = m
        l_scr[...] = l
        acc_scr[...] = acc

    m_scr[...] = jnp.full(m_scr.shape, -jnp.inf, F32)
    l_scr[...] = jnp.zeros(l_scr.shape, F32)
    acc_scr[...] = jnp.zeros(acc_scr.shape, F32)
    s_a[...] = logits(0)

    def pair(jp, carry):
        kc = 2 * jp
        s_b[...] = logits(kc + 1)
        update(s_a[...], kc)
        s_a[...] = logits(kc + 2)
        update(s_b[...], kc + 1)
        return carry

    lax.fori_loop(0, i // 2, pair, 0)
    key = lax.broadcasted_iota(jnp.int32, (tile, cols), 0)
    qry = lax.broadcasted_iota(jnp.int32, (tile, cols), 1) % tile

    @pl.when(i % 2 == 1)
    def _():
        s_b[...] = logits(i)
        update(s_a[...], i - 1)
        update(jnp.where(key <= qry, s_b[...], NEG), i)

    @pl.when(i % 2 == 0)
    def _():
        update(jnp.where(key <= qry, s_a[...], NEG), i)

    _store_heads(o_ref, acc_scr[...] * (_gate_row(gate_ref) / l_scr[...]), tile)


def _nsa_slc(qt, sbt, ske, svt, gates_t, tq):
    b, g, _, _, t = qt.shape
    gw = NSA_GROUP * HEAD_DIM
    nk, tk = svt.shape[2], svt.shape[4]
    assert tq == tk
    cols = NSA_GROUP * tq
    return pl.pallas_call(
        functools.partial(_nsa_slc_kernel, tile=tq),
        out_shape=jax.ShapeDtypeStruct((b, t, NSA_W), F32),
        grid=(b, g, t // tq),
        in_specs=[
            pl.BlockSpec((None, None, NSA_GROUP, HEAD_DIM, tq), lambda bi, gi, i: (bi, gi, 0, 0, i)),
            pl.BlockSpec((None, None, LANES, tq), lambda bi, gi, i: (bi, gi, 0, i)),
            pl.BlockSpec((None, None, t, 2 * HEAD_DIM), lambda bi, gi, i: (bi, gi, 0, 0)),
            pl.BlockSpec((None, None, nk, HEAD_DIM, tk), lambda bi, gi, i: (bi, gi, 0, 0, 0)),
            pl.BlockSpec((None, None, None, NSA_GROUP, tq), lambda bi, gi, i: (bi, 1, gi, 0, i)),
        ],
        out_specs=pl.BlockSpec((None, tq, gw), lambda bi, gi, i: (bi, i, gi)),
        scratch_shapes=[pltpu.VMEM((tk, cols), F32), pltpu.VMEM((tk, cols), F32),
                        pltpu.VMEM((1, cols), F32), pltpu.VMEM((1, cols), F32), pltpu.VMEM((HEAD_DIM, cols), F32)],
        compiler_params=_params(("parallel", "parallel", "arbitrary")),
        name="nsa_selected",
    )(qt, sbt, ske, svt, gates_t)


def _nsa_win_kernel(qt_ref, k_ref, vt_ref, gate_ref, ocmp_ref, oslc_ref, o_ref, *, tile):
    i = pl.program_id(2)
    q_t = _heads_on_lanes(qt_ref)
    cols = NSA_GROUP * tile
    back = WINDOW // tile
    chunks = [jnp.maximum(i - back + j, 0) for j in range(back + 1)]
    v_t = jnp.concatenate([vt_ref[c] for c in chunks], axis=1)
    r = lax.broadcasted_iota(jnp.int32, (tile, cols), 0)
    c = lax.broadcasted_iota(jnp.int32, (tile, cols), 1) % tile
    blocks = []
    for j in range(back + 1):
        sj = _dot(k_ref[pl.ds(pl.multiple_of(chunks[j] * tile, tile), tile), :], q_t)
        if j == 0:
            sj = jnp.where(r > c, sj, NEG)
        if j == back:
            sj = jnp.where(c >= r, sj, NEG)
        else:
            sj = sj + jnp.where(i - back + j >= 0, 0.0, NEG)
        blocks.append(sj)
    s = jnp.concatenate(blocks, axis=0)
    p = jnp.exp2(s - jnp.max(s, axis=0, keepdims=True))
    l = jnp.sum(p, axis=0, keepdims=True)
    o_t = _dot(v_t, p.astype(BF16)) * (_gate_row(gate_ref) / l)
    for j in range(NSA_GROUP):
        sl = slice(j * HEAD_DIM, (j + 1) * HEAD_DIM)
        o_ref[:, sl] = (ocmp_ref[:, sl] + oslc_ref[:, sl] + o_t[:, j * tile:(j + 1) * tile].T).astype(o_ref.dtype)


def _nsa_win(qt, wkr, wvt, gates_t, o_cmp, o_slc, tile):
    b, g, _, _, t = qt.shape
    gw = NSA_GROUP * HEAD_DIM
    nk = t // tile
    assert WINDOW % tile == 0 and WINDOW >= tile
    rows = pl.BlockSpec((None, tile, gw), lambda bi, gi, i: (bi, i, gi))
    return pl.pallas_call(
        functools.partial(_nsa_win_kernel, tile=tile),
        out_shape=jax.ShapeDtypeStruct((b, t, NSA_W), BF16),
        grid=(b, g, nk),
        in_specs=[
            pl.BlockSpec((None, None, NSA_GROUP, HEAD_DIM, tile), lambda bi, gi, i: (bi, gi, 0, 0, i)),
            pl.BlockSpec((None, None, t, HEAD_DIM), lambda bi, gi, i: (bi, gi, 0, 0)),
            pl.BlockSpec((None, None, nk, HEAD_DIM, tile), lambda bi, gi, i: (bi, gi, 0, 0, 0)),
            pl.BlockSpec((None, None, None, NSA_GROUP, tile), lambda bi, gi, i: (bi, 2, gi, 0, i)),
            rows, rows,
        ],
        out_specs=rows,
        compiler_params=_params(("parallel", "parallel", "parallel")),
        name="nsa_window",
    )(qt, wkr, wvt, gates_t, o_cmp, o_slc)


def _mem_attn_kernel(q_ref, k_ref, v_ref, o_ref, *, scale):
    s = _dot_nt(q_ref[...], k_ref[...]) * scale
    e = jnp.exp(s - jnp.max(s, axis=-1, keepdims=True))
    p = e / jnp.sum(e, axis=-1, keepdims=True)
    o_ref[...] = _dot(p.astype(BF16), v_ref[...]).astype(o_ref.dtype)


def _mem_attention(proj, mkv, off_mq, tq):
    b, t, _ = proj.shape
    m = mkv.shape[1]
    d = MEM_HEAD_DIM
    return pl.pallas_call(
        functools.partial(_mem_attn_kernel, scale=d ** -0.5),
        out_shape=jax.ShapeDtypeStruct((b, t, MEM_W), BF16),
        grid=(b, MEM_HEADS, t // tq),
        in_specs=[
            pl.BlockSpec((None, tq, d), lambda bi, h, i: (bi, i, off_mq // d + h)),
            pl.BlockSpec((None, m, d), lambda bi, h, i: (bi, 0, h)),
            pl.BlockSpec((None, m, d), lambda bi, h, i: (bi, 0, MEM_HEADS + h)),
        ],
        out_specs=pl.BlockSpec((None, tq, d), lambda bi, h, i: (bi, i, h)),
        compiler_params=_params(("parallel", "parallel", "parallel")),
        name="mem_attention",
    )(proj, mkv, mkv)


def _merge_kernel(ofox_ref, onsa_ref, omem_ref, bg0_ref, bg1_ref, bg2_ref, wf_ref, wn_ref, wm_ref, o_ref):
    sig = lambda r: _sigmoid(r[...].astype(F32))
    o_ref[...] = (sig(bg0_ref) * _dot(ofox_ref[...], wf_ref[...])
                  + sig(bg1_ref) * _dot(onsa_ref[...], wn_ref[...])
                  + sig(bg2_ref) * _dot(omem_ref[...], wm_ref[...])).astype(o_ref.dtype)


def _merge(ofox, onsa, omem, proj, wf, wn, wm, tm, tn):
    m, dm = ofox.shape[0], wf.shape[1]
    row = lambda w: pl.BlockSpec((tm, w), lambda j, i: (i, 0))
    gate = lambda br: pl.BlockSpec((tm, tn), lambda j, i: (i, (OFF_BG + br * dm) // tn + j))
    wspec = lambda w: pl.BlockSpec((w.shape[0], tn), lambda j, i: (0, j))
    return pl.pallas_call(
        _merge_kernel,
        out_shape=jax.ShapeDtypeStruct((m, dm), BF16),
        grid=(dm // tn, m // tm),
        in_specs=[row(FOX_W), row(NSA_W), row(MEM_W), gate(0), gate(1), gate(2), wspec(wf), wspec(wn), wspec(wm)],
        out_specs=pl.BlockSpec((tm, tn), lambda j, i: (i, j)),
        compiler_params=_params(("parallel", "parallel")),
        name="branch_merge",
    )(ofox, onsa, omem, proj, proj, proj, wf, wn, wm)


def _out_proj_kernel(m_ref, w_ref, g_ref, h_ref, o_ref):
    o_ref[...] = h_ref[...] + _rms(_dot(m_ref[...], w_ref[...]), g_ref[...])


def _out_proj(merged, w, g, h, tm):
    m, d = h.shape
    return pl.pallas_call(
        _out_proj_kernel,
        out_shape=jax.ShapeDtypeStruct((m, d), F32),
        grid=(m // tm,),
        in_specs=[
            pl.BlockSpec((tm, d), lambda i: (i, 0)),
            pl.BlockSpec((d, d), lambda i: (0, 0)),
            pl.BlockSpec((1, d), lambda i: (0, 0)),
            pl.BlockSpec((tm, d), lambda i: (i, 0)),
        ],
        out_specs=pl.BlockSpec((tm, d), lambda i: (i, 0)),
        compiler_params=_params(("parallel",)),
        name="out_proj",
    )(merged, w, g.reshape(1, d), h)


def _mlp_kernel(h_ref, gpre_ref, w1_ref, w2_ref, gpost_ref, o_ref, u_ref, acc_ref):
    f = pl.program_id(1)

    @pl.when(f == 0)
    def _():
        u_ref[...] = _rms(h_ref[...], gpre_ref[...]).astype(BF16)
        acc_ref[...] = jnp.zeros_like(acc_ref)

    a = jnp.square(jnp.maximum(_dot(u_ref[...], w1_ref[...]), 0.0))
    acc_ref[...] += _dot(a.astype(BF16), w2_ref[...])

    @pl.when(f == pl.num_programs(1) - 1)
    def _():
        o_ref[...] = h_ref[...] + _rms(acc_ref[...], gpost_ref[...])


def _mlp(h, gpre, w1, w2, gpost, tm, tf):
    m, d = h.shape
    dff = w1.shape[1]
    return pl.pallas_call(
        _mlp_kernel,
        out_shape=jax.ShapeDtypeStruct((m, d), F32),
        grid=(m // tm, dff // tf),
        in_specs=[
            pl.BlockSpec((tm, d), lambda i, f: (i, 0)),
            pl.BlockSpec((1, d), lambda i, f: (0, 0)),
            pl.BlockSpec((d, tf), lambda i, f: (0, f)),
            pl.BlockSpec((tf, d), lambda i, f: (f, 0)),
            pl.BlockSpec((1, d), lambda i, f: (0, 0)),
        ],
        out_specs=pl.BlockSpec((tm, d), lambda i, f: (i, 0)),
        scratch_shapes=[pltpu.VMEM((tm, d), BF16), pltpu.VMEM((tm, d), F32)],
        compiler_params=_params(("parallel", "arbitrary")),
        name="mlp",
    )(h, gpre.reshape(1, d), w1, w2, gpost.reshape(1, d))


def _rope_tables(t):
    half = ROPE_DIM // 2
    inv_freq = ROPE_THETA ** (-jnp.arange(half, dtype=F32) / half)
    ang = jnp.arange(t, dtype=jnp.int32).astype(F32)[:, None] * inv_freq[None, :]
    cos, sin = jnp.cos(ang), jnp.sin(ang)
    pad = HEAD_DIM - ROPE_DIM
    cos_t = jnp.concatenate([cos, cos, jnp.ones((t, pad), F32)], axis=1)
    sin_t = jnp.concatenate([-sin, sin, jnp.zeros((t, pad), F32)], axis=1)
    return cos_t, sin_t


def _w_in_kernel(x_ref, o_ref):
    o_ref[...] = x_ref[0].T.astype(o_ref.dtype)


def _rows_kernel(x_ref, o_ref):
    o_ref[...] = x_ref[0]


def _split_w_in(w_in_t, layer, d_model, tc):
    widths = (FOX_W, FOX_W, FOX_W, FOX_HEADS, NSA_W) + (NSA_KV_W,) * 6 + (3 * NSA_HEADS, MEM_W, N_BRANCHES * d_model)
    offs = [int(o) for o in np.concatenate([[0], np.cumsum(widths)])]
    runs = ((0, 3), (4, 5), (13, 14), (5, 11), (12, 13))
    n_main = sum(offs[last] - offs[first] for first, last in runs)
    _, _, d = w_in_t.shape
    bounds, shifts, dst = [], [], 0
    for first, last in runs:
        src, width = offs[first], offs[last] - offs[first]
        assert width % tc == 0 and dst % tc == 0 and (src - dst) % SUBLANES == 0
        dst += width
        bounds.append(dst // tc)
        shifts.append(src - (dst - width))

    def src_rows(j):
        shift = shifts[-1]
        for bound, s in zip(reversed(bounds[:-1]), reversed(shifts[:-1])):
            shift = jnp.where(j < bound, s, shift)
        return layer, pl.multiple_of(j * tc + shift, SUBLANES), 0

    main = pl.pallas_call(
        _w_in_kernel,
        out_shape=jax.ShapeDtypeStruct((d, n_main), BF16),
        grid=(n_main // tc,),
        in_specs=[pl.BlockSpec((pl.Element(1), pl.Element(tc), pl.Element(d)), src_rows)],
        out_specs=pl.BlockSpec((d, tc), lambda j: (0, j)),
        compiler_params=_params(("parallel",)),
        name="w_in_layout",
    )(w_in_t)

    def take_rows(start, count):
        return pl.pallas_call(
            _rows_kernel,
            out_shape=jax.ShapeDtypeStruct((count, d), w_in_t.dtype),
            grid=(1,),
            in_specs=[pl.BlockSpec((pl.Element(1), pl.Element(count), pl.Element(d)), lambda j: (layer, start, 0))],
            out_specs=pl.BlockSpec((count, d), lambda j: (0, 0)),
            name="w_in_rows",
        )(w_in_t)

    narrow = jnp.concatenate([take_rows(offs[3], widths[3]), take_rows(offs[11], widths[11]),
                              jnp.zeros((SMALL_W - widths[3] - widths[11], d), w_in_t.dtype)], axis=0)
    return main, narrow.T.astype(BF16)


def _pick(n, candidates):
    for c in candidates:
        if n % c == 0:
            return c
    raise ValueError(f"no tile for {n}")


class _Tiles(NamedTuple):
    proj_rows: int
    proj_cols: int
    w_in_cols: int
    fox_q: int
    fox_k: int
    nsa: int
    slc: int
    mem_rows: int
    mem_cols: int
    mem_q: int
    rows: int
    merge_cols: int
    mlp_hidden: int


def _tiles(n, t, n_mem, n_main, d_model, d_ff):
    nsa = _pick(t, (256, 128))
    slc = _pick(t, (2 * nsa, nsa))
    return _Tiles(
        proj_rows=_pick(n, (1024, 512, 256)),
        proj_cols=_pick(n_main, (1280, 1024, 512, 256, 128)),
        w_in_cols=512,
        fox_q=_pick(t, (2048, 1024, 512, 256, 128)),
        fox_k=_pick(t, (512, 256, 128)),
        nsa=nsa,
        slc=slc,
        mem_rows=_pick(n_mem, (1024, 512, 256)),
        mem_cols=_pick(2 * MEM_W, (1024, 512, 256)),
        mem_q=_pick(t, (1024, 512, 256, 128)),
        rows=_pick(n, (512, 256)),
        merge_cols=_pick(d_model, (1024, 512, 256)),
        mlp_hidden=_pick(d_ff, (1024, 512, 256)),
    )


def kernel(x, mem, w_in, b_f, w_cmp1_k, w_cmp2_k, pe_cmp_k, w_cmp1_v, w_cmp2_v, pe_cmp_v, w_mem_kv, g_mem, w_up_fox, w_up_nsa, w_up_mem, w_o, g_pre_mix, g_post_mix, g_pre_mlp, g_post_mlp, w_mlp1, w_mlp2):
    b, t, d = x.shape
    depth = w_in.shape[0]
    n = b * t
    mlen = mem.shape[1]
    off_kv = OFF_BG + N_BRANCHES * d
    off_mq = off_kv + 6 * NSA_KV_W
    n_main = off_mq + MEM_W

    cos_t, sin_t = _rope_tables(t)
    tiles = _tiles(n, t, b * mlen, n_main, d, w_mlp1.shape[2])

    h = x.reshape(n, d)
    mem2 = mem.reshape(b * mlen, d)
    w_in_t = jnp.swapaxes(w_in, 1, 2)
    for l in range(depth):
        w_main, w_small = _split_w_in(w_in_t, l, d, tiles.w_in_cols)
        proj, small = _norm_matmul(h, g_pre_mix[l], w_main, w_small, tiles.proj_rows, tiles.proj_cols, "in_proj")
        proj = proj.reshape(b, t, n_main)

        ff = small[:, SMALL_FF:SMALL_FF + FOX_HEADS].reshape(b, t, FOX_HEADS).transpose(0, 2, 1)
        c = _fox_cumsum(ff, b_f[l])
        o_fox = _fox_attention(proj, c, tiles.fox_q, tiles.fox_k)

        kc, vc, ske, wkr, svt, wvt = _nsa_prep(
            proj, cos_t, sin_t,
            w_cmp1_k[l].astype(BF16), w_cmp2_k[l].astype(BF16), pe_cmp_k[l],
            w_cmp1_v[l].astype(BF16), w_cmp2_v[l].astype(BF16), pe_cmp_v[l], off_kv, tiles.slc, tiles.nsa)
        ng = small[:, SMALL_NG:SMALL_NG + 3 * NSA_HEADS].reshape(b, t, 3, NSA_KV_HEADS, NSA_GROUP)
        gates_cmp = ng[:, :, 0].transpose(0, 2, 1, 3)
        gates_t = ng.transpose(0, 2, 3, 4, 1)
        qt, o_cmp, sbt = _nsa_cmp(proj, cos_t, sin_t, kc, vc, gates_cmp, tiles.nsa)
        o_slc = _nsa_slc(qt, sbt, ske, svt, gates_t, tiles.slc)
        o_nsa = _nsa_win(qt, wkr, wvt, gates_t, o_cmp, o_slc, tiles.nsa)

        half = lambda w: _layer_bf16(w, l, w.shape[1] // 2)
        (mkv,) = _norm_matmul(mem2, g_mem[l], half(w_mem_kv), None, tiles.mem_rows, tiles.mem_cols, "mem_kv")
        o_mem = _mem_attention(proj, mkv.reshape(b, mlen, 2 * MEM_W), off_mq, tiles.mem_q)

        merged = _merge(o_fox.reshape(n, FOX_W), o_nsa.reshape(n, NSA_W), o_mem.reshape(n, MEM_W), proj.reshape(n, n_main),
                        half(w_up_fox), half(w_up_nsa), half(w_up_mem), tiles.rows, tiles.merge_cols)
        h = _out_proj(merged, half(w_o), g_post_mix[l], h, tiles.rows)
        w1 = _layer_bf16(w_mlp1, l, w_mlp1.shape[1] // 8)
        w2 = _layer_bf16(w_mlp2, l, w_mlp2.shape[1] // 8)
        h = _mlp(h, g_pre_mlp[l], w1, w2, g_post_mlp[l], tiles.rows, tiles.mlp_hidden)
    return h.reshape(b, t, d)
```

```python
import functools
from typing import NamedTuple

import numpy as np
import jax
import jax.numpy as jnp
from jax import lax
from jax.experimental import pallas as pl
from jax.experimental.pallas import tpu as pltpu

F32 = jnp.float32
BF16 = jnp.bfloat16

HEAD_DIM = 128
FOX_HEADS = 8
NSA_HEADS = 8
NSA_KV_HEADS = 2
NSA_GROUP = NSA_HEADS // NSA_KV_HEADS
MEM_HEADS = 4
MEM_HEAD_DIM = 256
N_BRANCHES = 3
CMP_LEN = 32
CMP_STRIDE = 16
SLC_BLOCK = 64
SLC_TOPK = 16
WINDOW = 512
ROPE_THETA = 500000.0
ROPE_DIM = HEAD_DIM // 4
RMS_EPS = 1e-6
NEG = -1e30
FORCED_SCORE = 1e6
LOG2E = 1.4426950408889634

LANES = 128
SUBLANES = 8
VMEM_LIMIT = 56 * 1024 * 1024

FOX_W = FOX_HEADS * HEAD_DIM
NSA_W = NSA_HEADS * HEAD_DIM
NSA_KV_W = NSA_KV_HEADS * HEAD_DIM
MEM_W = MEM_HEADS * MEM_HEAD_DIM
OFF_FQ = 0
OFF_FK = OFF_FQ + FOX_W
OFF_FV = OFF_FK + FOX_W
OFF_NQ = OFF_FV + FOX_W
OFF_BG = OFF_NQ + NSA_W
SMALL_W = LANES
SMALL_FF = 0
SMALL_NG = FOX_HEADS


def _params(sem):
    return pltpu.CompilerParams(dimension_semantics=sem, vmem_limit_bytes=VMEM_LIMIT)


def _rms(x, g):
    return (x * lax.rsqrt(jnp.mean(x * x, axis=-1, keepdims=True) + RMS_EPS)) * g


def _sigmoid(x):
    return 0.5 * jnp.tanh(0.5 * x) + 0.5


def _dot(a, b):
    return jnp.dot(a, b, preferred_element_type=F32)


def _dot_nt(a, b):
    return lax.dot_general(a, b, (((1,), (1,)), ((), ())), preferred_element_type=F32)


def _cast_kernel(x_ref, o_ref):
    o_ref[...] = x_ref[...].astype(o_ref.dtype)


def _layer_bf16(w, layer, tr):
    _, r, c = w.shape
    return pl.pallas_call(
        _cast_kernel,
        out_shape=jax.ShapeDtypeStruct((r, c), BF16),
        grid=(r // tr,),
        in_specs=[pl.BlockSpec((None, tr, c), lambda i: (layer, i, 0))],
        out_specs=pl.BlockSpec((tr, c), lambda i: (i, 0)),
        compiler_params=_params(("parallel",)),
        name="weight_bf16",
    )(w)


def _norm_matmul_kernel(x_ref, g_ref, w_ref, *rest, narrow):
    if narrow:
        ws_ref, o_ref, os_ref, u_ref = rest
    else:
        o_ref, u_ref = rest

    @pl.when(pl.program_id(1) == 0)
    def _():
        u_ref[...] = _rms(x_ref[...], g_ref[...]).astype(BF16)
        if narrow:
            os_ref[...] = _dot(u_ref[...], ws_ref[...])

    o_ref[...] = _dot(u_ref[...], w_ref[...]).astype(o_ref.dtype)


def _norm_matmul(x, g, w, ws, tm, tn, name):
    m, d = x.shape
    n = w.shape[1]
    narrow = ws is not None
    in_specs = [
        pl.BlockSpec((tm, d), lambda i, j: (i, 0)),
        pl.BlockSpec((1, d), lambda i, j: (0, 0)),
        pl.BlockSpec((d, tn), lambda i, j: (0, j)),
    ]
    out_shape = [jax.ShapeDtypeStruct((m, n), BF16)]
    out_specs = [pl.BlockSpec((tm, tn), lambda i, j: (i, j))]
    args = [x, g.reshape(1, d), w]
    if narrow:
        in_specs.append(pl.BlockSpec((d, SMALL_W), lambda i, j: (0, 0)))
        out_shape.append(jax.ShapeDtypeStruct((m, SMALL_W), F32))
        out_specs.append(pl.BlockSpec((tm, SMALL_W), lambda i, j: (i, 0)))
        args.append(ws)
    return pl.pallas_call(
        functools.partial(_norm_matmul_kernel, narrow=narrow),
        out_shape=tuple(out_shape),
        grid=(m // tm, n // tn),
        in_specs=in_specs,
        out_specs=tuple(out_specs),
        scratch_shapes=[pltpu.VMEM((tm, d), BF16)],
        compiler_params=_params(("parallel", "arbitrary")),
        name=name,
    )(*args)


def _fox_cumsum_kernel(bf_ref, ff_ref, c_ref):
    heads, rows, lanes = ff_ref.shape
    lane = lax.broadcasted_iota(jnp.int32, (rows, lanes), 1)
    row = lax.broadcasted_iota(jnp.int32, (rows, lanes), 0)
    for hh in range(heads):
        x = ff_ref[hh] + bf_ref[hh]
        y = jnp.minimum(x, 0.0) - jnp.log1p(jnp.exp(-jnp.abs(x)))
        s = 1
        while s < lanes:
            y = y + jnp.where(lane >= s, pltpu.roll(y, s, axis=1), 0.0)
            s *= 2
        tot = jnp.broadcast_to(y[:, lanes - 1:lanes], y.shape)
        z = tot
        s = 1
        while s < rows:
            z = z + jnp.where(row >= s, pltpu.roll(z, s, axis=0), 0.0)
            s *= 2
        c_ref[hh] = (y + (z - tot)) * LOG2E


def _fox_cumsum(ff, b_f):
    b, h, t = ff.shape
    rows = t // LANES
    out = pl.pallas_call(
        _fox_cumsum_kernel,
        out_shape=jax.ShapeDtypeStruct((b, h, rows, LANES), F32),
        grid=(b,),
        in_specs=[
            pl.BlockSpec(memory_space=pltpu.SMEM),
            pl.BlockSpec((None, h, rows, LANES), lambda i: (i, 0, 0, 0)),
        ],
        out_specs=pl.BlockSpec((None, h, rows, LANES), lambda i: (i, 0, 0, 0)),
        compiler_params=_params(("parallel",)),
        name="fox_cumsum",
    )(b_f, ff.reshape(b, h, rows, LANES))
    return out.reshape(b, h, t)


def _flash_t_step(carry, s, v_t):
    m, l, acc = carry
    m_new = jnp.maximum(m, jnp.max(s, axis=0, keepdims=True))
    alpha = jnp.exp2(m - m_new)
    p = jnp.exp2(s - m_new)
    l = alpha * l + jnp.sum(p, axis=0, keepdims=True)
    acc = alpha * acc + _dot(v_t, p.astype(BF16))
    return m_new, l, acc


FOX_C_TERMS = 3


def _fox_attn_kernel(q_ref, k_ref, v_ref, c_ref, o_ref, ke_scr, vt_scr, s_a, s_b, m_scr, l_scr, acc_scr,
                     *, tq, tk, n_tiles, qscale):
    i = pl.program_id(2)
    nk = vt_scr.shape[0]
    per_tile = tq // tk

    @pl.when(i == 0)
    def _():
        row = lax.broadcasted_iota(jnp.int32, (HEAD_DIM, tk), 0)
        for kc in range(nk):
            rows = slice(kc * tk, (kc + 1) * tk)
            vt_scr[kc] = v_ref[rows, :].astype(F32).T.astype(BF16)
            ke_scr[rows, 0:HEAD_DIM] = k_ref[rows, :]
            c = c_ref[:, rows]
            terms = jnp.zeros((HEAD_DIM, tk), F32)
            for n in range(FOX_C_TERMS):
                part = c.astype(BF16).astype(F32)
                terms = jnp.where(row == n, part, terms)
                c = c - part
            ke_scr[rows, HEAD_DIM:2 * HEAD_DIM] = terms.T.astype(BF16)

    minus_ones = jnp.where(lax.broadcasted_iota(jnp.int32, (HEAD_DIM, tq), 0) < FOX_C_TERMS, -1.0, 0.0).astype(BF16)
    q_t = jnp.concatenate([(q_ref[...].astype(F32) * qscale).T.astype(BF16), minus_ones], axis=0)

    def logits(kc, lo):
        return _dot(ke_scr[kc * tk:(kc + 1) * tk, :], q_t[:, lo:])

    def run(tile):
        chunks = [(kc, 0, False) for kc in range(tile * per_tile)]
        chunks += [(tile * per_tile + d, d * tk, True) for d in range(per_tile)]
        m_scr[...] = jnp.full(m_scr.shape, -jnp.inf, F32)
        l_scr[...] = jnp.zeros(l_scr.shape, F32)
        acc_scr[...] = jnp.zeros(acc_scr.shape, F32)
        bufs = (s_a, s_b)
        bufs[0][:, chunks[0][1]:] = logits(chunks[0][0], chunks[0][1])
        for idx, (kc, lo, masked) in enumerate(chunks):
            if idx + 1 < len(chunks):
                kn, ln, _ = chunks[idx + 1]
                bufs[(idx + 1) % 2][:, ln:] = logits(kn, ln)
            s = bufs[idx % 2][:, lo:]
            if masked:
                key = lax.broadcasted_iota(jnp.int32, (tk, tq - lo), 0)
                col = lax.broadcasted_iota(jnp.int32, (tk, tq - lo), 1)
                s = jnp.where(key <= col, s, NEG)
            m, l, acc = _flash_t_step((m_scr[:, lo:], l_scr[:, lo:], acc_scr[:, lo:]), s, vt_scr[kc])
            m_scr[:, lo:] = m
            l_scr[:, lo:] = l
            acc_scr[:, lo:] = acc
        o_ref[...] = (acc_scr[...] / l_scr[...]).T.astype(o_ref.dtype)

    for tile in range(n_tiles):
        pl.when(i == tile)(functools.partial(run, tile))


def _fox_attention(proj, c2, tq, tk):
    b, t, _ = proj.shape
    nk = t // tk
    n_tiles = t // tq
    assert tq % tk == 0
    qb, kb, vb = OFF_FQ // HEAD_DIM, OFF_FK // HEAD_DIM, OFF_FV // HEAD_DIM
    return pl.pallas_call(
        functools.partial(_fox_attn_kernel, tq=tq, tk=tk, n_tiles=n_tiles, qscale=HEAD_DIM ** -0.5 * LOG2E),
        out_shape=jax.ShapeDtypeStruct((b, t, FOX_W), BF16),
        grid=(b, FOX_HEADS, n_tiles),
        in_specs=[
            pl.BlockSpec((None, tq, HEAD_DIM), lambda bi, h, i: (bi, i, qb + h)),
            pl.BlockSpec((None, t, HEAD_DIM), lambda bi, h, i: (bi, 0, kb + h)),
            pl.BlockSpec((None, t, HEAD_DIM), lambda bi, h, i: (bi, 0, vb + h)),
            pl.BlockSpec((None, None, 1, t), lambda bi, h, i: (bi, h, 0, 0)),
        ],
        out_specs=pl.BlockSpec((None, tq, HEAD_DIM), lambda bi, h, i: (bi, i, h)),
        scratch_shapes=[
            pltpu.VMEM((t, 2 * HEAD_DIM), BF16), pltpu.VMEM((nk, HEAD_DIM, tk), BF16),
            pltpu.VMEM((tk, tq), F32), pltpu.VMEM((tk, tq), F32),
            pltpu.VMEM((1, tq), F32), pltpu.VMEM((1, tq), F32), pltpu.VMEM((HEAD_DIM, tq), F32),
        ],
        compiler_params=_params(("parallel", "parallel", "arbitrary")),
        name="fox_attention",
    )(proj, proj, proj, c2.reshape(b, FOX_HEADS, 1, t))


def _rope(x, cos, sin):
    half = ROPE_DIM // 2
    lane = lax.broadcasted_iota(jnp.int32, x.shape, 1)
    swapped = jnp.where(lane < half, pltpu.roll(x, LANES - half, axis=1), pltpu.roll(x, half, axis=1))
    return x * cos + swapped * sin


def _compress(xs_ref, x, w1_ref, w2_ref, pe_ref):
    t = x.shape[0]
    nch = t // CMP_STRIDE
    xs_ref[...] = x
    pe = pe_ref[...]
    first, second = [], []
    for p in range(CMP_STRIDE):
        xp = xs_ref[pl.ds(p, nch, stride=CMP_STRIDE), :]
        first.append((xp + pe[p:p + 1]).astype(BF16))
        second.append((xp + pe[CMP_STRIDE + p:CMP_STRIDE + p + 1]).astype(BF16))
    kw = CMP_STRIDE * HEAD_DIM
    a = _dot(jnp.concatenate(first, axis=1), w1_ref[0:kw, :])
    b = _dot(jnp.concatenate(second, axis=1), w1_ref[kw:2 * kw, :])
    hidden = jax.nn.gelu(a + pltpu.roll(b, nch - 1, axis=0))
    return _dot(hidden.astype(BF16), w2_ref[...])


def _store_transposed_chunks(dst_ref, src_ref):
    nk, _, tile = dst_ref.shape
    for c in range(nk):
        dst_ref[c] = src_ref[c * tile:(c + 1) * tile, :].astype(F32).T.astype(dst_ref.dtype)


def _nsa_prep_kernel(ck_ref, cv_ref, sk_ref, sv_ref, wk_ref, wv_ref, cos_ref, sin_ref,
                     w1k_ref, w2k_ref, pek_ref, w1v_ref, w2v_ref, pev_ref,
                     kc_ref, vc_ref, ske_ref, wkr_ref, svt_ref, wvt_ref, xs_ref):
    cos = cos_ref[...]
    sin = sin_ref[...]
    t = cos.shape[0]
    _store_transposed_chunks(svt_ref, sv_ref)
    _store_transposed_chunks(wvt_ref, wv_ref)
    kc_ref[...] = _compress(xs_ref, _rope(ck_ref[...].astype(F32), cos, sin), w1k_ref, w2k_ref, pek_ref).astype(BF16)
    vc_ref[...] = _compress(xs_ref, cv_ref[...].astype(F32), w1v_ref, w2v_ref, pev_ref).astype(BF16)
    ske_ref[:, 0:HEAD_DIM] = _rope(sk_ref[...].astype(F32), cos, sin).astype(BF16)
    row = lax.broadcasted_iota(jnp.int32, (t, LANES), 0)
    lane = lax.broadcasted_iota(jnp.int32, (t, LANES), 1)
    ske_ref[:, HEAD_DIM:2 * HEAD_DIM] = jnp.where(row // SLC_BLOCK == lane, 1.0, 0.0).astype(BF16)
    wkr_ref[...] = _rope(wk_ref[...].astype(F32), cos, sin).astype(BF16)


def _nsa_prep(proj, cos, sin, w1k, w2k, pek, w1v, w2v, pev, off_kv, slc_tk, win_tk):
    b, t, _ = proj.shape
    g = NSA_KV_HEADS
    nch = t // CMP_STRIDE
    blk = lambda sec: (lambda bi, gi: (bi, 0, (off_kv + sec * NSA_KV_W) // HEAD_DIM + gi))
    full = lambda shape: pl.BlockSpec(shape, lambda bi, gi: (0,) * len(shape))
    per_head = lambda shape: pl.BlockSpec((None, None) + shape, lambda bi, gi: (bi, gi) + (0,) * len(shape))
    return pl.pallas_call(
        _nsa_prep_kernel,
        out_shape=(
            jax.ShapeDtypeStruct((b, g, nch, HEAD_DIM), BF16),
            jax.ShapeDtypeStruct((b, g, nch, HEAD_DIM), BF16),
            jax.ShapeDtypeStruct((b, g, t, 2 * HEAD_DIM), BF16),
            jax.ShapeDtypeStruct((b, g, t, HEAD_DIM), BF16),
            jax.ShapeDtypeStruct((b, g, t // slc_tk, HEAD_DIM, slc_tk), BF16),
            jax.ShapeDtypeStruct((b, g, t // win_tk, HEAD_DIM, win_tk), BF16),
        ),
        grid=(b, g),
        in_specs=[pl.BlockSpec((None, t, HEAD_DIM), blk(sec)) for sec in range(6)] + [
            full((t, LANES)), full((t, LANES)),
            full(w1k.shape), full(w2k.shape), full(pek.shape),
            full(w1v.shape), full(w2v.shape), full(pev.shape),
        ],
        out_specs=(
            per_head((nch, HEAD_DIM)), per_head((nch, HEAD_DIM)),
            per_head((t, 2 * HEAD_DIM)), per_head((t, HEAD_DIM)),
            per_head((t // slc_tk, HEAD_DIM, slc_tk)), per_head((t // win_tk, HEAD_DIM, win_tk)),
        ),
        scratch_shapes=[pltpu.VMEM((t, HEAD_DIM), F32)],
        compiler_params=_params(("parallel", "parallel")),
        name="nsa_prep",
    )(proj, proj, proj, proj, proj, proj, cos, sin, w1k, w2k, pek, w1v, w2v, pev)


def _nsa_cmp_kernel(q_ref, cos_ref, sin_ref, kc_ref, vc_ref, gate_ref, qt_ref, ocmp_ref, sbt_ref, *, tq, scale):
    q0 = pl.program_id(2) * tq
    gates = _sigmoid(gate_ref[...])
    cos = cos_ref[...]
    sin = sin_ref[...]
    kc = kc_ref[...]
    vc = vc_ref[...]
    nch = kc.shape[0]
    n_real = nch - 1
    nsel = (nch * CMP_STRIDE) // SLC_BLOCK
    tpos = q0 + lax.broadcasted_iota(jnp.int32, (tq, nch), 0)
    nidx = lax.broadcasted_iota(jnp.int32, (tq, nch), 1)
    valid = (nidx * CMP_STRIDE + (CMP_LEN - 1) <= tpos) & (nidx < n_real)

    psum = jnp.zeros((tq, nch), F32)
    for j in range(NSA_GROUP):
        sl = slice(j * HEAD_DIM, (j + 1) * HEAD_DIM)
        qf = _rope(q_ref[:, sl].astype(F32), cos, sin)
        qt_ref[j] = (qf * (scale * LOG2E)).T.astype(BF16)
        qj = qf.astype(BF16)
        lg = jnp.where(valid, _dot_nt(qj, kc) * scale, NEG)
        e = jnp.where(valid, jnp.exp(lg - jnp.max(lg, axis=-1, keepdims=True)), 0.0)
        den = jnp.sum(e, axis=-1, keepdims=True)
        inv = 1.0 / jnp.where(den > 0.0, den, 1.0)
        ocmp_ref[:, sl] = _dot((e * (inv * gates[:, j:j + 1])).astype(BF16), vc).astype(ocmp_ref.dtype)
        psum = psum + e * inv

    ratio = SLC_BLOCK // CMP_STRIDE
    jrow = lax.broadcasted_iota(jnp.int32, (nsel, nch), 0)
    ncol = lax.broadcasted_iota(jnp.int32, (nsel, nch), 1)
    lo_n = jrow * ratio - (CMP_LEN - 1) // CMP_STRIDE
    hi_n = jrow * ratio + (SLC_BLOCK - 1) // CMP_STRIDE
    overlap_t = jnp.where((ncol >= lo_n) & (ncol <= hi_n) & (ncol < n_real), 1.0, 0.0).astype(BF16)
    p_hi = psum.astype(BF16)
    r1 = psum - p_hi.astype(F32)
    p_mid = r1.astype(BF16)
    p_lo = (r1 - p_mid.astype(F32)).astype(BF16)
    pslc_t = _dot_nt(overlap_t, p_hi) + _dot_nt(overlap_t, p_mid) + _dot_nt(overlap_t, p_lo)

    blk = lax.broadcasted_iota(jnp.int32, (nsel, tq), 0)
    cur = (q0 + lax.broadcasted_iota(jnp.int32, (nsel, tq), 1)) // SLC_BLOCK
    forced = (blk == 0) | (blk == cur) | (blk == cur - 1)
    score = jnp.where(forced, FORCED_SCORE, jnp.where(blk <= cur, pslc_t, -1.0))

    sub = SUBLANES
    n_groups = nsel // sub
    groups = [score[g * sub:(g + 1) * sub, :] for g in range(n_groups)]
    ranks = [jnp.zeros((sub, tq), F32) for _ in range(n_groups)]
    row_in_group = lax.broadcasted_iota(jnp.int32, (sub, tq), 0)
    for i in range(nsel):
        gi, ri = divmod(i, sub)
        si = score[i:i + 1, :]
        for g in range(n_groups):
            if g > gi:
                beats = si >= groups[g]
            elif g < gi:
                beats = si > groups[g]
            else:
                beats = (si > groups[g]) | ((si == groups[g]) & (row_in_group > ri))
            ranks[g] = ranks[g] + jnp.where(beats, 1.0, 0.0)
    rank = jnp.concatenate(ranks, axis=0)
    k_top = min(SLC_TOPK, nsel)
    bias_t = jnp.where(rank < k_top, 0.0, NEG)
    sbt_ref[...] = jnp.concatenate([bias_t, jnp.zeros((LANES - nsel, tq), F32)], axis=0).astype(BF16)


def _nsa_cmp(proj, cos, sin, kc, vc, gates_cmp, tq):
    b, t, _ = proj.shape
    g = NSA_KV_HEADS
    gw = NSA_GROUP * HEAD_DIM
    nch = kc.shape[2]
    assert (nch * CMP_STRIDE) // SLC_BLOCK <= LANES // 2
    return pl.pallas_call(
        functools.partial(_nsa_cmp_kernel, tq=tq, scale=HEAD_DIM ** -0.5),
        out_shape=(
            jax.ShapeDtypeStruct((b, g, NSA_GROUP, HEAD_DIM, t), BF16),
            jax.ShapeDtypeStruct((b, t, NSA_W), F32),
            jax.ShapeDtypeStruct((b, g, LANES, t), BF16),
        ),
        grid=(b, g, t // tq),
        in_specs=[
            pl.BlockSpec((None, tq, gw), lambda bi, gi, i: (bi, i, OFF_NQ // gw + gi)),
            pl.BlockSpec((tq, LANES), lambda bi, gi, i: (i, 0)),
            pl.BlockSpec((tq, LANES), lambda bi, gi, i: (i, 0)),
            pl.BlockSpec((None, None, nch, HEAD_DIM), lambda bi, gi, i: (bi, gi, 0, 0)),
            pl.BlockSpec((None, None, nch, HEAD_DIM), lambda bi, gi, i: (bi, gi, 0, 0)),
            pl.BlockSpec((None, None, tq, NSA_GROUP), lambda bi, gi, i: (bi, gi, i, 0)),
        ],
        out_specs=(
            pl.BlockSpec((None, None, NSA_GROUP, HEAD_DIM, tq), lambda bi, gi, i: (bi, gi, 0, 0, i)),
            pl.BlockSpec((None, tq, gw), lambda bi, gi, i: (bi, i, gi)),
            pl.BlockSpec((None, None, LANES, tq), lambda bi, gi, i: (bi, gi, 0, i)),
        ),
        compiler_params=_params(("parallel", "parallel", "parallel")),
        name="nsa_cmp_select",
    )(proj, cos, sin, kc, vc, gates_cmp)


def _heads_on_lanes(qt_ref, extra=None):
    parts = []
    for j in range(NSA_GROUP):
        parts.append(qt_ref[j] if extra is None else jnp.concatenate([qt_ref[j], extra], axis=0))
    return jnp.concatenate(parts, axis=1)


def _store_heads(o_ref, o_t, tile):
    for j in range(NSA_GROUP):
        o_ref[:, j * HEAD_DIM:(j + 1) * HEAD_DIM] = o_t[:, j * tile:(j + 1) * tile].T.astype(o_ref.dtype)


def _gate_row(gate_ref):
    g = _sigmoid(gate_ref[...])
    return jnp.concatenate([g[j:j + 1, :] for j in range(NSA_GROUP)], axis=1)


def _nsa_slc_kernel(qt_ref, sbt_ref, ke_ref, vt_ref, gate_ref, o_ref, s_a, s_b, m_scr, l_scr, acc_scr,
                    *, tile, tiles_per_step, n_steps):
    cols = NSA_GROUP * tile
    key = lax.broadcasted_iota(jnp.int32, (tile, cols), 0)
    qry = lax.broadcasted_iota(jnp.int32, (tile, cols), 1) % tile

    def run(step):
        first = step * tiles_per_step
        seq = [(t, kc) for t in range(first, first + tiles_per_step) for kc in range(t + 1)]

        def logits(t, kc):
            qs = slice((t - first) * tile, (t - first + 1) * tile)
            q_t = jnp.concatenate([jnp.concatenate([qt_ref[j, :, qs], sbt_ref[:, qs]], axis=0)
                                   for j in range(NSA_GROUP)], axis=1)
            return _dot(ke_ref[kc * tile:(kc + 1) * tile, :], q_t)

        bufs = (s_a, s_b)
        bufs[0][...] = logits(*seq[0])
        for idx, (t, kc) in enumerate(seq):
            if kc == 0:
                m_scr[...] = jnp.full(m_scr.shape, -jnp.inf, F32)
                l_scr[...] = jnp.zeros(l_scr.shape, F32)
                acc_scr[...] = jnp.zeros(acc_scr.shape, F32)
            if idx + 1 < len(seq):
                bufs[(idx + 1) % 2][...] = logits(*seq[idx + 1])
            s = bufs[idx % 2][...]
            if kc == t:
                s = jnp.where(key <= qry, s, NEG)
            m, l, acc = _flash_t_step((m_scr[...], l_scr[...], acc_scr[...]), s, vt_ref[kc])
            if kc < t:
                m_scr[...] = m
                l_scr[...] = l
                acc_scr[...] = acc
            else:
                qs = slice((t - first) * tile, (t - first + 1) * tile)
                g = _sigmoid(gate_ref[:, qs])
                gate_row = jnp.concatenate([g[j:j + 1, :] for j in range(NSA_GROUP)], axis=1)
                o_t = acc * (gate_row / l)
                for j in range(NSA_GROUP):
                    o_ref[qs, j * HEAD_DIM:(j + 1) * HEAD_DIM] = o_t[:, j * tile:(j + 1) * tile].T.astype(o_ref.dtype)

    for step in range(n_steps):
        pl.when(pl.program_id(2) == step)(functools.partial(run, step))


def _nsa_slc(qt, sbt, ske, svt, gates_t, tq):
    b, g, _, _, t = qt.shape
    gw = NSA_GROUP * HEAD_DIM
    nk, tk = svt.shape[2], svt.shape[4]
    assert tq == tk
    cols = NSA_GROUP * tq
    n_steps = 2 if (t // tq) % 2 == 0 else 1
    span = t // n_steps
    return pl.pallas_call(
        functools.partial(_nsa_slc_kernel, tile=tq, tiles_per_step=span // tq, n_steps=n_steps),
        out_shape=jax.ShapeDtypeStruct((b, t, NSA_W), F32),
        grid=(b, g, n_steps),
        in_specs=[
            pl.BlockSpec((None, None, NSA_GROUP, HEAD_DIM, span), lambda bi, gi, i: (bi, gi, 0, 0, i)),
            pl.BlockSpec((None, None, LANES, span), lambda bi, gi, i: (bi, gi, 0, i)),
            pl.BlockSpec((None, None, t, 2 * HEAD_DIM), lambda bi, gi, i: (bi, gi, 0, 0)),
            pl.BlockSpec((None, None, nk, HEAD_DIM, tk), lambda bi, gi, i: (bi, gi, 0, 0, 0)),
            pl.BlockSpec((None, None, None, NSA_GROUP, span), lambda bi, gi, i: (bi, 1, gi, 0, i)),
        ],
        out_specs=pl.BlockSpec((None, span, gw), lambda bi, gi, i: (bi, i, gi)),
        scratch_shapes=[pltpu.VMEM((tk, cols), F32), pltpu.VMEM((tk, cols), F32),
                        pltpu.VMEM((1, cols), F32), pltpu.VMEM((1, cols), F32), pltpu.VMEM((HEAD_DIM, cols), F32)],
        compiler_params=_params(("parallel", "parallel", "arbitrary")),
        name="nsa_selected",
    )(qt, sbt, ske, svt, gates_t)


def _nsa_win_kernel(qt_ref, k_ref, vt_ref, gate_ref, ocmp_ref, oslc_ref, o_ref, *, tile):
    i = pl.program_id(2)
    q_t = _heads_on_lanes(qt_ref)
    cols = NSA_GROUP * tile
    back = WINDOW // tile
    chunks = [jnp.maximum(i - back + j, 0) for j in range(back + 1)]
    v_t = jnp.concatenate([vt_ref[c] for c in chunks], axis=1)
    r = lax.broadcasted_iota(jnp.int32, (tile, cols), 0)
    c = lax.broadcasted_iota(jnp.int32, (tile, cols), 1) % tile
    blocks = []
    for j in range(back + 1):
        sj = _dot(k_ref[pl.ds(pl.multiple_of(chunks[j] * tile, tile), tile), :], q_t)
        if j == 0:
            sj = jnp.where(r > c, sj, NEG)
        if j == back:
            sj = jnp.where(c >= r, sj, NEG)
        else:
            sj = sj + jnp.where(i - back + j >= 0, 0.0, NEG)
        blocks.append(sj)
    s = jnp.concatenate(blocks, axis=0)
    p = jnp.exp2(s - jnp.max(s, axis=0, keepdims=True))
    l = jnp.sum(p, axis=0, keepdims=True)
    o_t = _dot(v_t, p.astype(BF16)) * (_gate_row(gate_ref) / l)
    for j in range(NSA_GROUP):
        sl = slice(j * HEAD_DIM, (j + 1) * HEAD_DIM)
        o_ref[:, sl] = (ocmp_ref[:, sl] + oslc_ref[:, sl] + o_t[:, j * tile:(j + 1) * tile].T).astype(o_ref.dtype)


def _nsa_win(qt, wkr, wvt, gates_t, o_cmp, o_slc, tile):
    b, g, _, _, t = qt.shape
    gw = NSA_GROUP * HEAD_DIM
    nk = t // tile
    assert WINDOW % tile == 0 and WINDOW >= tile
    rows = pl.BlockSpec((None, tile, gw), lambda bi, gi, i: (bi, i, gi))
    return pl.pallas_call(
        functools.partial(_nsa_win_kernel, tile=tile),
        out_shape=jax.ShapeDtypeStruct((b, t, NSA_W), BF16),
        grid=(b, g, nk),
        in_specs=[
            pl.BlockSpec((None, None, NSA_GROUP, HEAD_DIM, tile), lambda bi, gi, i: (bi, gi, 0, 0, i)),
            pl.BlockSpec((None, None, t, HEAD_DIM), lambda bi, gi, i: (bi, gi, 0, 0)),
            pl.BlockSpec((None, None, nk, HEAD_DIM, tile), lambda bi, gi, i: (bi, gi, 0, 0, 0)),
            pl.BlockSpec((None, None, None, NSA_GROUP, tile), lambda bi, gi, i: (bi, 2, gi, 0, i)),
            rows, rows,
        ],
        out_specs=rows,
        compiler_params=_params(("parallel", "parallel", "parallel")),
        name="nsa_window",
    )(qt, wkr, wvt, gates_t, o_cmp, o_slc)


def _mem_attn_kernel(q_ref, k_ref, v_ref, o_ref, *, scale):
    s = _dot_nt(q_ref[...], k_ref[...]) * scale
    e = jnp.exp(s - jnp.max(s, axis=-1, keepdims=True))
    p = e / jnp.sum(e, axis=-1, keepdims=True)
    o_ref[...] = _dot(p.astype(BF16), v_ref[...]).astype(o_ref.dtype)


def _mem_attention(proj, mkv, off_mq, tq):
    b, t, _ = proj.shape
    m = mkv.shape[1]
    d = MEM_HEAD_DIM
    return pl.pallas_call(
        functools.partial(_mem_attn_kernel, scale=d ** -0.5),
        out_shape=jax.ShapeDtypeStruct((b, t, MEM_W), BF16),
        grid=(b, MEM_HEADS, t // tq),
        in_specs=[
            pl.BlockSpec((None, tq, d), lambda bi, h, i: (bi, i, off_mq // d + h)),
            pl.BlockSpec((None, m, d), lambda bi, h, i: (bi, 0, h)),
            pl.BlockSpec((None, m, d), lambda bi, h, i: (bi, 0, MEM_HEADS + h)),
        ],
        out_specs=pl.BlockSpec((None, tq, d), lambda bi, h, i: (bi, i, h)),
        compiler_params=_params(("parallel", "parallel", "parallel")),
        name="mem_attention",
    )(proj, mkv, mkv)


def _merge_kernel(ofox_ref, onsa_ref, omem_ref, bg0_ref, bg1_ref, bg2_ref, wf_ref, wn_ref, wm_ref, o_ref):
    sig = lambda r: _sigmoid(r[...].astype(F32))
    o_ref[...] = (sig(bg0_ref) * _dot(ofox_ref[...], wf_ref[...])
                  + sig(bg1_ref) * _dot(onsa_ref[...], wn_ref[...])
                  + sig(bg2_ref) * _dot(omem_ref[...], wm_ref[...])).astype(o_ref.dtype)


def _merge(ofox, onsa, omem, proj, wf, wn, wm, tm, tn):
    m, dm = ofox.shape[0], wf.shape[1]
    row = lambda w: pl.BlockSpec((tm, w), lambda j, i: (i, 0))
    gate = lambda br: pl.BlockSpec((tm, tn), lambda j, i: (i, (OFF_BG + br * dm) // tn + j))
    wspec = lambda w: pl.BlockSpec((w.shape[0], tn), lambda j, i: (0, j))
    return pl.pallas_call(
        _merge_kernel,
        out_shape=jax.ShapeDtypeStruct((m, dm), BF16),
        grid=(dm // tn, m // tm),
        in_specs=[row(FOX_W), row(NSA_W), row(MEM_W), gate(0), gate(1), gate(2), wspec(wf), wspec(wn), wspec(wm)],
        out_specs=pl.BlockSpec((tm, tn), lambda j, i: (i, j)),
        compiler_params=_params(("parallel", "parallel")),
        name="branch_merge",
    )(ofox, onsa, omem, proj, proj, proj, wf, wn, wm)


def _out_proj_kernel(m_ref, w_ref, g_ref, h_ref, o_ref):
    o_ref[...] = h_ref[...] + _rms(_dot(m_ref[...], w_ref[...]), g_ref[...])


def _out_proj(merged, w, g, h, tm):
    m, d = h.shape
    return pl.pallas_call(
        _out_proj_kernel,
        out_shape=jax.ShapeDtypeStruct((m, d), F32),
        grid=(m // tm,),
        in_specs=[
            pl.BlockSpec((tm, d), lambda i: (i, 0)),
            pl.BlockSpec((d, d), lambda i: (0, 0)),
            pl.BlockSpec((1, d), lambda i: (0, 0)),
            pl.BlockSpec((tm, d), lambda i: (i, 0)),
        ],
        out_specs=pl.BlockSpec((tm, d), lambda i: (i, 0)),
        compiler_params=_params(("parallel",)),
        name="out_proj",
    )(merged, w, g.reshape(1, d), h)


def _mlp_kernel(h_ref, gpre_ref, w1_ref, w2_ref, gpost_ref, o_ref, u_ref, acc_ref):
    f = pl.program_id(1)

    @pl.when(f == 0)
    def _():
        u_ref[...] = _rms(h_ref[...], gpre_ref[...]).astype(BF16)
        acc_ref[...] = jnp.zeros_like(acc_ref)

    a = jnp.square(jnp.maximum(_dot(u_ref[...], w1_ref[...]), 0.0))
    acc_ref[...] += _dot(a.astype(BF16), w2_ref[...])

    @pl.when(f == pl.num_programs(1) - 1)
    def _():
        o_ref[...] = h_ref[...] + _rms(acc_ref[...], gpost_ref[...])


def _mlp(h, gpre, w1, w2, gpost, tm, tf):
    m, d = h.shape
    dff = w1.shape[1]
    return pl.pallas_call(
        _mlp_kernel,
        out_shape=jax.ShapeDtypeStruct((m, d), F32),
        grid=(m // tm, dff // tf),
        in_specs=[
            pl.BlockSpec((tm, d), lambda i, f: (i, 0)),
            pl.BlockSpec((1, d), lambda i, f: (0, 0)),
            pl.BlockSpec((d, tf), lambda i, f: (0, f)),
            pl.BlockSpec((tf, d), lambda i, f: (f, 0)),
            pl.BlockSpec((1, d), lambda i, f: (0, 0)),
        ],
        out_specs=pl.BlockSpec((tm, d), lambda i, f: (i, 0)),
        scratch_shapes=[pltpu.VMEM((tm, d), BF16), pltpu.VMEM((tm, d), F32)],
        compiler_params=_params(("parallel", "arbitrary")),
        name="mlp",
    )(h, gpre.reshape(1, d), w1, w2, gpost.reshape(1, d))


def _rope_tables(t):
    half = ROPE_DIM // 2
    inv_freq = ROPE_THETA ** (-jnp.arange(half, dtype=F32) / half)
    ang = jnp.arange(t, dtype=jnp.int32).astype(F32)[:, None] * inv_freq[None, :]
    cos, sin = jnp.cos(ang), jnp.sin(ang)
    pad = HEAD_DIM - ROPE_DIM
    cos_t = jnp.concatenate([cos, cos, jnp.ones((t, pad), F32)], axis=1)
    sin_t = jnp.concatenate([-sin, sin, jnp.zeros((t, pad), F32)], axis=1)
    return cos_t, sin_t


def _w_in_kernel(x_ref, o_ref):
    o_ref[...] = x_ref[0].T.astype(o_ref.dtype)


def _rows_kernel(x_ref, o_ref):
    o_ref[...] = x_ref[0]


def _split_w_in(w_in_t, layer, d_model, tc):
    widths = (FOX_W, FOX_W, FOX_W, FOX_HEADS, NSA_W) + (NSA_KV_W,) * 6 + (3 * NSA_HEADS, MEM_W, N_BRANCHES * d_model)
    offs = [int(o) for o in np.concatenate([[0], np.cumsum(widths)])]
    runs = ((0, 3), (4, 5), (13, 14), (5, 11), (12, 13))
    n_main = sum(offs[last] - offs[first] for first, last in runs)
    _, _, d = w_in_t.shape
    bounds, shifts, dst = [], [], 0
    for first, last in runs:
        src, width = offs[first], offs[last] - offs[first]
        assert width % tc == 0 and dst % tc == 0 and (src - dst) % SUBLANES == 0
        dst += width
        bounds.append(dst // tc)
        shifts.append(src - (dst - width))

    def src_rows(j):
        shift = shifts[-1]
        for bound, s in zip(reversed(bounds[:-1]), reversed(shifts[:-1])):
            shift = jnp.where(j < bound, s, shift)
        return layer, pl.multiple_of(j * tc + shift, SUBLANES), 0

    main = pl.pallas_call(
        _w_in_kernel,
        out_shape=jax.ShapeDtypeStruct((d, n_main), BF16),
        grid=(n_main // tc,),
        in_specs=[pl.BlockSpec((pl.Element(1), pl.Element(tc), pl.Element(d)), src_rows)],
        out_specs=pl.BlockSpec((d, tc), lambda j: (0, j)),
        compiler_params=_params(("parallel",)),
        name="w_in_layout",
    )(w_in_t)

    def take_rows(start, count):
        return pl.pallas_call(
            _rows_kernel,
            out_shape=jax.ShapeDtypeStruct((count, d), w_in_t.dtype),
            grid=(1,),
            in_specs=[pl.BlockSpec((pl.Element(1), pl.Element(count), pl.Element(d)), lambda j: (layer, start, 0))],
            out_specs=pl.BlockSpec((count, d), lambda j: (0, 0)),
            name="w_in_rows",
        )(w_in_t)

    narrow = jnp.concatenate([take_rows(offs[3], widths[3]), take_rows(offs[11], widths[11]),
                              jnp.zeros((SMALL_W - widths[3] - widths[11], d), w_in_t.dtype)], axis=0)
    return main, narrow.T.astype(BF16)


def _pick(n, candidates):
    for c in candidates:
        if n % c == 0:
            return c
    raise ValueError(f"no tile for {n}")


class _Tiles(NamedTuple):
    proj_rows: int
    proj_cols: int
    w_in_cols: int
    fox_q: int
    fox_k: int
    nsa: int
    slc: int
    mem_rows: int
    mem_cols: int
    mem_q: int
    rows: int
    merge_cols: int
    mlp_hidden: int


def _tiles(n, t, n_mem, n_main, d_model, d_ff):
    nsa = _pick(t, (256, 128))
    slc = _pick(t, (2 * nsa, nsa))
    return _Tiles(
        proj_rows=_pick(n, (1024, 512, 256)),
        proj_cols=_pick(n_main, (1280, 1024, 512, 256, 128)),
        w_in_cols=512,
        fox_q=_pick(t, (2048, 1024, 512, 256, 128)),
        fox_k=_pick(t, (512, 256, 128)),
        nsa=nsa,
        slc=slc,
        mem_rows=_pick(n_mem, (1024, 512, 256)),
        mem_cols=_pick(2 * MEM_W, (1024, 512, 256)),
        mem_q=_pick(t, (1024, 512, 256, 128)),
        rows=_pick(n, (512, 256)),
        merge_cols=_pick(d_model, (1024, 512, 256)),
        mlp_hidden=_pick(d_ff, (1024, 512, 256)),
    )


def kernel(x, mem, w_in, b_f, w_cmp1_k, w_cmp2_k, pe_cmp_k, w_cmp1_v, w_cmp2_v, pe_cmp_v, w_mem_kv, g_mem, w_up_fox, w_up_nsa, w_up_mem, w_o, g_pre_mix, g_post_mix, g_pre_mlp, g_post_mlp, w_mlp1, w_mlp2):
    b, t, d = x.shape
    depth = w_in.shape[0]
    n = b * t
    mlen = mem.shape[1]
    off_kv = OFF_BG + N_BRANCHES * d
    off_mq = off_kv + 6 * NSA_KV_W
    n_main = off_mq + MEM_W

    cos_t, sin_t = _rope_tables(t)
    tiles = _tiles(n, t, b * mlen, n_main, d, w_mlp1.shape[2])

    h = x.reshape(n, d)
    mem2 = mem.reshape(b * mlen, d)
    w_in_t = jnp.swapaxes(w_in, 1, 2)
    for l in range(depth):
        w_main, w_small = _split_w_in(w_in_t, l, d, tiles.w_in_cols)
        proj, small = _norm_matmul(h, g_pre_mix[l], w_main, w_small, tiles.proj_rows, tiles.proj_cols, "in_proj")
        proj = proj.reshape(b, t, n_main)

        ff = small[:, SMALL_FF:SMALL_FF + FOX_HEADS].reshape(b, t, FOX_HEADS).transpose(0, 2, 1)
        c = _fox_cumsum(ff, b_f[l])
        o_fox = _fox_attention(proj, c, tiles.fox_q, tiles.fox_k)

        kc, vc, ske, wkr, svt, wvt = _nsa_prep(
            proj, cos_t, sin_t,
            w_cmp1_k[l].astype(BF16), w_cmp2_k[l].astype(BF16), pe_cmp_k[l],
            w_cmp1_v[l].astype(BF16), w_cmp2_v[l].astype(BF16), pe_cmp_v[l], off_kv, tiles.slc, tiles.nsa)
        ng = small[:, SMALL_NG:SMALL_NG + 3 * NSA_HEADS].reshape(b, t, 3, NSA_KV_HEADS, NSA_GROUP)
        gates_cmp = ng[:, :, 0].transpose(0, 2, 1, 3)
        gates_t = ng.transpose(0, 2, 3, 4, 1)
        qt, o_cmp, sbt = _nsa_cmp(proj, cos_t, sin_t, kc, vc, gates_cmp, tiles.nsa)
        o_slc = _nsa_slc(qt, sbt, ske, svt, gates_t, tiles.slc)
        o_nsa = _nsa_win(qt, wkr, wvt, gates_t, o_cmp, o_slc, tiles.nsa)

        half = lambda w: _layer_bf16(w, l, w.shape[1] // 2)
        (mkv,) = _norm_matmul(mem2, g_mem[l], half(w_mem_kv), None, tiles.mem_rows, tiles.mem_cols, "mem_kv")
        o_mem = _mem_attention(proj, mkv.reshape(b, mlen, 2 * MEM_W), off_mq, tiles.mem_q)

        merged = _merge(o_fox.reshape(n, FOX_W), o_nsa.reshape(n, NSA_W), o_mem.reshape(n, MEM_W), proj.reshape(n, n_main),
                        half(w_up_fox), half(w_up_nsa), half(w_up_mem), tiles.rows, tiles.merge_cols)
        h = _out_proj(merged, half(w_o), g_post_mix[l], h, tiles.rows)
        w1 = _layer_bf16(w_mlp1, l, w_mlp1.shape[1] // 8)
        w2 = _layer_bf16(w_mlp2, l, w_mlp2.shape[1] // 8)
        h = _mlp(h, g_pre_mlp[l], w1, w2, g_post_mlp[l], tiles.rows, tiles.mlp_hidden)
    return h.reshape(b, t, d)
```
